```python
import math
import jax, jax.numpy as jnp
from jax import lax
import numpy as np

D_MODEL = 4096
BATCH = 4
SEQ = 4096
DEPTH = 1

GRID_W = 64
SSM_EXPAND = 2
SSM_D_INNER = SSM_EXPAND * D_MODEL
SSM_HEAD_DIM = 64
SSM_HEADS = SSM_D_INNER // SSM_HEAD_DIM
SSM_GROUPS = 8
SSM_STATE = 128
SSM_CONV_W = 3
SSM_CHUNK = 128
SSM_CONV_DIM = SSM_D_INNER + 2 * SSM_GROUPS * SSM_STATE
ATTN_HEAD_DIM = 128
ATTN_Q_HEADS = D_MODEL // ATTN_HEAD_DIM
ATTN_KV_HEADS = 8
ATTN_BLOCK_Q = 128
ROPE_THETA = 10000.0
FFN_DIM = 11008
FFN_CONV_W = 3
LN_EPS = 1e-5
RMS_EPS = 1e-6
DEEPNORM_ALPHA = (2.0 * DEPTH) ** 0.25
DEEPNORM_BETA = (8.0 * DEPTH) ** -0.25
ADA_INIT = 0.5

Z_COLS = SSM_D_INNER
XBC_COLS = SSM_CONV_DIM
DT_COLS = 2 * SSM_HEADS
Q_COLS = ATTN_Q_HEADS * ATTN_HEAD_DIM
KV_COLS = ATTN_KV_HEADS * ATTN_HEAD_DIM
IN_COLS = Z_COLS + XBC_COLS + DT_COLS + Q_COLS + 2 * KV_COLS
IN_SPLITS = [Z_COLS, Z_COLS + XBC_COLS, Z_COLS + XBC_COLS + DT_COLS,
             Z_COLS + XBC_COLS + DT_COLS + Q_COLS,
             Z_COLS + XBC_COLS + DT_COLS + Q_COLS + KV_COLS]

kernel_name = "hybrid_ssd_gqa_convffn_deepnorm_adaln"


def layer_norm(x, g=None, b=None):
    xf = x.astype(jnp.float32)
    mu = jnp.mean(xf, axis=-1, keepdims=True)
    var = jnp.mean(jnp.square(xf - mu), axis=-1, keepdims=True)
    y = (xf - mu) * lax.rsqrt(var + LN_EPS)
    if g is not None:
        y = y * g.astype(jnp.float32) + b.astype(jnp.float32)
    return y.astype(x.dtype)


def rms_norm(x, g):
    xf = x.astype(jnp.float32)
    y = xf * lax.rsqrt(jnp.mean(jnp.square(xf), axis=-1, keepdims=True) + RMS_EPS)
    return (y * g.astype(jnp.float32)).astype(x.dtype)


def dwconv_centered(x, w, b):
    K = w.shape[0]
    pad = K // 2
    S = x.shape[1]
    xp = jnp.pad(x, ((0, 0), (pad, pad), (0, 0)))
    y = xp[:, 0:S, :] * w[0]
    for j in range(1, K):
        y = y + xp[:, j:j + S, :] * w[j]
    return y + b


def axial_rope(x, row, col):
    half = x.shape[-1] // 2
    quarter = half // 2
    inv = 1.0 / (ROPE_THETA ** (jnp.arange(quarter, dtype=jnp.float32) / quarter))

    def rot(xh, pos):
        ang = pos.astype(jnp.float32)[:, None] * inv[None, :]
        cos = jnp.cos(ang)[None, :, None, :]
        sin = jnp.sin(ang)[None, :, None, :]
        x1 = xh[..., :quarter].astype(jnp.float32)
        x2 = xh[..., quarter:].astype(jnp.float32)
        return jnp.concatenate([x1 * cos - x2 * sin, x2 * cos + x1 * sin], axis=-1)

    out = jnp.concatenate([rot(x[..., :half], row), rot(x[..., half:], col)], axis=-1)
    return out.astype(x.dtype)


def block_attention(q, k, v):
    Bsz, S, Hq, Dh = q.shape
    Hkv = k.shape[2]
    G = Hq // Hkv
    nb = S // ATTN_BLOCK_Q
    qb = q.reshape(Bsz, nb, ATTN_BLOCK_Q, Hkv, G, Dh).transpose(1, 0, 2, 3, 4, 5)
    scale = Dh ** -0.5

    def one_block(qi):
        s = jnp.einsum('bqkgd,bskd->bkgqs', qi, k).astype(jnp.float32) * scale
        p = jax.nn.softmax(s, axis=-1).astype(v.dtype)
        return jnp.einsum('bkgqs,bskd->bqkgd', p, v)

    o = lax.map(one_block, qb)
    return o.transpose(1, 0, 2, 3, 4, 5).reshape(Bsz, S, Hq * Dh)


def ssd_scan(x, dt, A, Bm, Cm):
    Bsz, S, H, P = x.shape
    G, N = Bm.shape[2], Bm.shape[3]
    Hg = H // G
    L = SSM_CHUNK
    nc = S // L

    def chunk(t):
        return jnp.moveaxis(t.reshape((Bsz, nc, L) + t.shape[2:]), 1, 0)

    xc = chunk(x.reshape(Bsz, S, G, Hg, P))
    ac = chunk((dt * A).reshape(Bsz, S, G, Hg))
    dtc = chunk(dt.reshape(Bsz, S, G, Hg))
    Bc = chunk(Bm)
    Cc = chunk(Cm)
    mask = jnp.tril(jnp.ones((L, L), dtype=bool))[None, :, :, None, None]

    def step(h, inp):
        xi, ai, dti, Bi, Ci = inp
        cum = jnp.cumsum(ai, axis=1)
        seg = cum[:, :, None] - cum[:, None, :]
        decay = jnp.exp(jnp.where(mask, seg, -jnp.inf))
        cb = jnp.einsum('blgn,bsgn->blsg', Ci, Bi)
        y_intra = jnp.einsum('blsg,blsgh,bsgh,bsghp->blghp', cb, decay, dti, xi)
        y_inter = jnp.einsum('blgn,bghpn,blgh->blghp', Ci, h, jnp.exp(cum))
        to_end = jnp.exp(cum[:, -1:] - cum) * dti
        h_new = h * jnp.exp(cum[:, -1])[..., None, None] + \
            jnp.einsum('bsgn,bsgh,bsghp->bghpn', Bi, to_end, xi)
        return h_new, y_intra + y_inter

    h0 = jnp.zeros((Bsz, G, Hg, P, N), jnp.float32)
    _, y = lax.scan(step, h0, (xc, ac, dtc, Bc, Cc))
    return jnp.moveaxis(y, 0, 1).reshape(Bsz, S, H, P)


def ssd_branch(z, xbc, dt_raw, conv_w, conv_b, a_log, dt_bias, d_skip, norm_w):
    Bsz, S, _ = z.shape
    f32 = jnp.float32
    xbc = jax.nn.silu(dwconv_centered(xbc, conv_w, conv_b)).astype(f32)
    xs = xbc[..., :SSM_D_INNER].reshape(Bsz, S, SSM_HEADS, SSM_HEAD_DIM)
    Bm = xbc[..., SSM_D_INNER:SSM_D_INNER + SSM_GROUPS * SSM_STATE].reshape(Bsz, S, SSM_GROUPS, SSM_STATE)
    Cm = xbc[..., SSM_D_INNER + SSM_GROUPS * SSM_STATE:].reshape(Bsz, S, SSM_GROUPS, SSM_STATE)
    dt = jax.nn.softplus(dt_raw.astype(f32).reshape(Bsz, S, 2, SSM_HEADS) + dt_bias.astype(f32))
    A = -jnp.exp(a_log.astype(f32))
    flip = lambda t: jnp.flip(t, axis=1)
    y_f = ssd_scan(xs, dt[:, :, 0], A[0], Bm, Cm)
    y_b = flip(ssd_scan(flip(xs), flip(dt[:, :, 1]), A[1], flip(Bm), flip(Cm)))
    y = (y_f + y_b + d_skip.astype(f32)[:, None] * xs).reshape(Bsz, S, SSM_D_INNER)
    yg = (y * jax.nn.silu(z.astype(f32))).reshape(Bsz, S, SSM_GROUPS, SSM_D_INNER // SSM_GROUPS)
    yg = yg * lax.rsqrt(jnp.mean(jnp.square(yg), axis=-1, keepdims=True) + RMS_EPS)
    return (yg.reshape(Bsz, S, SSM_D_INNER) * norm_w.astype(f32)).astype(z.dtype)


def setup_inputs(seed: int = 0) -> dict:
    key = jax.random.key(seed)
    ks = jax.random.split(key, 26)
    f32 = jnp.float32
    nrm = lambda k, shape, s: jax.random.normal(k, shape, f32) * s
    dt0 = jnp.exp(jax.random.uniform(ks[7], (DEPTH, 2, SSM_HEADS), f32, math.log(1e-3), math.log(1e-1)))
    return {
        "x": nrm(ks[0], (BATCH, SEQ, D_MODEL), 1.0),
        "c": nrm(ks[1], (BATCH, D_MODEL), 1.0),
        "w_ada": nrm(ks[2], (DEPTH, D_MODEL, 6 * D_MODEL), ADA_INIT * D_MODEL ** -0.5),
        "b_ada": nrm(ks[3], (DEPTH, 6 * D_MODEL), 0.01),
        "w_in": nrm(ks[4], (DEPTH, D_MODEL, IN_COLS), D_MODEL ** -0.5),
        "ssm_conv_w": nrm(ks[5], (DEPTH, SSM_CONV_W, SSM_CONV_DIM), SSM_CONV_W ** -0.5),
        "ssm_conv_b": nrm(ks[6], (DEPTH, SSM_CONV_DIM), 0.01),
        "ssm_a_log": jnp.log(jax.random.uniform(ks[8], (DEPTH, 2, SSM_HEADS), f32, 1.0, 16.0)),
        "ssm_dt_bias": dt0 + jnp.log(-jnp.expm1(-dt0)),
        "ssm_d": 1.0 + nrm(ks[9], (DEPTH, SSM_HEADS), 0.1),
        "ssm_norm_w": 1.0 + nrm(ks[10], (DEPTH, SSM_D_INNER), 0.1),
        "q_norm_w": 1.0 + nrm(ks[11], (DEPTH, ATTN_HEAD_DIM), 0.1),
        "k_norm_w": 1.0 + nrm(ks[12], (DEPTH, ATTN_HEAD_DIM), 0.1),
        "w_ssm_proj": nrm(ks[13], (DEPTH, SSM_D_INNER, D_MODEL), SSM_D_INNER ** -0.5),
        "w_attn_proj": nrm(ks[14], (DEPTH, Q_COLS, D_MODEL), Q_COLS ** -0.5),
        "w_gate": nrm(ks[15], (DEPTH, D_MODEL, 2 * D_MODEL), D_MODEL ** -0.5),
        "b_gate": nrm(ks[16], (DEPTH, 2 * D_MODEL), 0.01),
        "w_out": nrm(ks[17], (DEPTH, D_MODEL, D_MODEL), DEEPNORM_BETA * D_MODEL ** -0.5),
        "ln1_g": 1.0 + nrm(ks[18], (DEPTH, D_MODEL), 0.1),
        "ln1_b": nrm(ks[19], (DEPTH, D_MODEL), 0.01),
        "w_up": nrm(ks[20], (DEPTH, D_MODEL, 2 * FFN_DIM), D_MODEL ** -0.5),
        "ffn_conv_w": nrm(ks[21], (DEPTH, FFN_CONV_W, 2 * FFN_DIM), FFN_CONV_W ** -0.5),
        "ffn_conv_b": nrm(ks[22], (DEPTH, 2 * FFN_DIM), 0.01),
        "w_down": nrm(ks[23], (DEPTH, FFN_DIM, D_MODEL), DEEPNORM_BETA * FFN_DIM ** -0.5),
        "ln2_g": 1.0 + nrm(ks[24], (DEPTH, D_MODEL), 0.1),
        "ln2_b": nrm(ks[25], (DEPTH, D_MODEL), 0.01),
    }


def reference(x, c, w_ada, b_ada, w_in, ssm_conv_w, ssm_conv_b, ssm_a_log, ssm_dt_bias,
              ssm_d, ssm_norm_w, q_norm_w, k_norm_w, w_ssm_proj, w_attn_proj, w_gate, b_gate,
              w_out, ln1_g, ln1_b, w_up, ffn_conv_w, ffn_conv_b, w_down, ln2_g, ln2_b):
    Bsz, S, D = x.shape
    rows = S // GRID_W
    row = jnp.repeat(jnp.arange(rows, dtype=jnp.int32), GRID_W)
    col = jnp.tile(jnp.arange(GRID_W, dtype=jnp.int32), rows)
    cond = jax.nn.silu(c)

    for layer in range(DEPTH):
        mod = (cond @ w_ada[layer] + b_ada[layer])[:, None, :]
        sh1, sc1, g1, sh2, sc2, g2 = jnp.split(mod, 6, axis=-1)

        h = layer_norm(x) * (1.0 + sc1) + sh1
        proj = h @ w_in[layer]
        z, xbc, dt_raw, q, k, v = jnp.split(proj, IN_SPLITS, axis=-1)

        y_ssm = ssd_branch(z, xbc, dt_raw, ssm_conv_w[layer], ssm_conv_b[layer],
                           ssm_a_log[layer], ssm_dt_bias[layer], ssm_d[layer],
                           ssm_norm_w[layer])

        q = rms_norm(q.reshape(Bsz, S, ATTN_Q_HEADS, ATTN_HEAD_DIM), q_norm_w[layer])
        k = rms_norm(k.reshape(Bsz, S, ATTN_KV_HEADS, ATTN_HEAD_DIM), k_norm_w[layer])
        v = v.reshape(Bsz, S, ATTN_KV_HEADS, ATTN_HEAD_DIM)
        q = axial_rope(q, row, col)
        k = axial_rope(k, row, col)
        y_attn = block_attention(q, k, v)

        o_ssm = y_ssm @ w_ssm_proj[layer]
        o_attn = y_attn @ w_attn_proj[layer]
        gates = jax.nn.sigmoid(h @ w_gate[layer] + b_gate[layer])
        ga, gb = jnp.split(gates, 2, axis=-1)
        mixed = (ga * o_ssm + gb * o_attn) @ w_out[layer]
        x = layer_norm(DEEPNORM_ALPHA * x + g1 * mixed, ln1_g[layer], ln1_b[layer])

        h2 = layer_norm(x) * (1.0 + sc2) + sh2
        u = dwconv_centered(h2 @ w_up[layer], ffn_conv_w[layer], ffn_conv_b[layer])
        ua, ub = jnp.split(u, 2, axis=-1)
        f = (jax.nn.silu(ua) * ub) @ w_down[layer]
        x = layer_norm(DEEPNORM_ALPHA * x + g2 * f, ln2_g[layer], ln2_b[layer])

    return x
```

```python
import functools

import jax
import jax.numpy as jnp
from jax import lax
from jax.experimental import pallas as pl
from jax.experimental.pallas import tpu as pltpu

F32 = jnp.float32
BF16 = jnp.bfloat16

GRID_W = 64
SSM_STATE = 128
SSM_CHUNK = 128
ROPE_THETA = 10000.0
LN_EPS = 1e-5
RMS_EPS = 1e-6

V7X_LANES = 128
V7X_BF16_SUBLANES = 16
V7X_VMEM_BYTES = 64 * 1024 * 1024
VMEM_CEILING = V7X_VMEM_BYTES - 6 * 1024 * 1024


def _pick(n, cands):
    for c in cands:
        if n % c == 0:
            return c
    return n


def _params(sem, est_bytes):
    limit = int(min(max(est_bytes, 32 * 1024 * 1024), VMEM_CEILING))
    return pltpu.CompilerParams(dimension_semantics=sem, vmem_limit_bytes=limit)


def _nbytes(shape, dtype):
    n = 1
    for s in shape:
        n *= s
    return n * jnp.dtype(dtype).itemsize


def _sigmoid(v):
    return 1.0 / (1.0 + jnp.exp(-v))


def _silu(v):
    return v * _sigmoid(v)


def _layer_norm(v):
    mu = jnp.mean(v, axis=-1, keepdims=True)
    vc = v - mu
    var = jnp.mean(vc * vc, axis=-1, keepdims=True)
    return vc * lax.rsqrt(var + LN_EPS)


def _mm_body(*refs, n_pairs, n_extras, prologue, epilogue):
    a_refs = refs[:n_pairs]
    b_refs = refs[n_pairs:2 * n_pairs]
    e_refs = refs[2 * n_pairs:2 * n_pairs + n_extras]
    o_refs = refs[2 * n_pairs + n_extras:]
    accs = []
    for a_ref, b_ref in zip(a_refs, b_refs):
        a = a_ref[...]
        if prologue is not None:
            a = prologue(a)
        accs.append(jnp.dot(a, b_ref[...].astype(BF16), preferred_element_type=F32))
    outs = epilogue(accs, [e[...] for e in e_refs])
    for o_ref, v in zip(o_refs, outs):
        o_ref[...] = v.astype(o_ref.dtype)


def _mm_vmem(tm, tn, a_list, b_list, n_extras, out_dtypes, has_prologue):
    est = sum(2 * _nbytes((tm, a.shape[1]), a.dtype) for a in a_list)
    est += sum(2 * _nbytes((b.shape[0], tn), b.dtype) for b in b_list)
    est += n_extras * 2 * _nbytes((tm, tn), F32)
    est += sum(2 * _nbytes((tm, tn), dt) for dt in out_dtypes)
    est += (len(a_list) + 2) * _nbytes((tm, tn), F32)
    if has_prologue or any(b.dtype != BF16 for b in b_list):
        est += max(_nbytes((b.shape[0], tn), BF16) for b in b_list) + _nbytes((tm, a_list[0].shape[1]), F32)
    return est


def _fit_tm(tm, tn, a_list, b_list, n_extras, out_dtypes):
    while tm > V7X_BF16_SUBLANES and _mm_vmem(tm, tn, a_list, b_list, n_extras, out_dtypes, False) > VMEM_CEILING:
        tm //= 2
    return tm


def _fused_matmul(a_list, b_list, extras, extra_specs, epilogue, out_dtypes, tm, tn, prologue=None):
    m = a_list[0].shape[0]
    n = b_list[0].shape[1]
    assert m % tm == 0 and n % tn == 0
    in_specs = ([pl.BlockSpec((tm, a.shape[1]), lambda i, j: (i, 0)) for a in a_list]
                + [pl.BlockSpec((b.shape[0], tn), lambda i, j: (0, j)) for b in b_list]
                + list(extra_specs))
    out_specs = [pl.BlockSpec((tm, tn), lambda i, j: (i, j)) for _ in out_dtypes]
    est = _mm_vmem(tm, tn, a_list, b_list, len(extras), out_dtypes, prologue is not None)
    body = functools.partial(_mm_body, n_pairs=len(a_list), n_extras=len(extras),
                             prologue=prologue, epilogue=epilogue)
    outs = pl.pallas_call(
        body,
        grid=(m // tm, n // tn),
        in_specs=in_specs,
        out_specs=out_specs,
        out_shape=[jax.ShapeDtypeStruct((m, n), dt) for dt in out_dtypes],
        compiler_params=_params(("parallel", "arbitrary"), est),
    )(*a_list, *b_list, *extras)
    return outs


def _ada_mod(c, w_ada, b_ada):
    bsz, d = c.shape
    rows = 8
    c_pad = jnp.zeros((rows, d), F32).at[:bsz].set(c)
    n = w_ada.shape[1]
    tn = _pick(n, (512, 256, 128))

    def prologue(a):
        return _silu(a).astype(BF16)

    def epilogue(accs, extras):
        return [accs[0] + extras[0]]

    (mod,) = _fused_matmul([c_pad], [w_ada], [b_ada.reshape(1, n)],
                           [pl.BlockSpec((1, tn), lambda i, j: (0, j))],
                           epilogue, [F32], rows, tn, prologue=prologue)
    return mod[:bsz]


def _ln_mod_body(x_ref, sh_ref, sc_ref, o_ref):
    y = _layer_norm(x_ref[...])
    o_ref[...] = (y * (1.0 + sc_ref[...]) + sh_ref[...]).astype(o_ref.dtype)


def _ln_modulate(x2, mod3, seq, shift_idx, scale_idx):
    t, d = x2.shape
    tm = _pick(seq, (256, 128, 64, 32, 16))
    per_seq = seq // tm
    est = 2 * _nbytes((tm, d), F32) + 2 * _nbytes((tm, d), BF16) + 4 * _nbytes((tm, d), F32)
    return pl.pallas_call(
        _ln_mod_body,
        grid=(t // tm,),
        in_specs=[pl.BlockSpec((tm, d), lambda i: (i, 0)),
                  pl.BlockSpec((None, 1, d), lambda i: (i // per_seq, 0, shift_idx)),
                  pl.BlockSpec((None, 1, d), lambda i: (i // per_seq, 0, scale_idx))],
        out_specs=pl.BlockSpec((tm, d), lambda i: (i, 0)),
        out_shape=jax.ShapeDtypeStruct((t, d), BF16),
        compiler_params=_params(("parallel",), est),
    )(x2, mod3, mod3)


def _res_ln_body(*refs, alpha, with_h):
    if with_h:
        x_ref, m_ref, gate_ref, g_ref, b_ref, sh_ref, sc_ref, o_ref, h_ref = refs
    else:
        x_ref, m_ref, gate_ref, g_ref, b_ref, o_ref = refs
    v = alpha * x_ref[...] + gate_ref[...] * m_ref[...].astype(F32)
    xn = _layer_norm(v) * g_ref[...] + b_ref[...]
    o_ref[...] = xn
    if with_h:
        h_ref[...] = (_layer_norm(xn) * (1.0 + sc_ref[...]) + sh_ref[...]).astype(h_ref.dtype)


def _residual_ln(x2, m2, mod3, gate_idx, ln_g, ln_b, seq, alpha, h_idx=None):
    t, d = x2.shape
    tm = _pick(seq, (256, 128, 64, 32, 16))
    per_seq = seq // tm
    with_h = h_idx is not None
    row = pl.BlockSpec((tm, d), lambda i: (i, 0))
    vec = pl.BlockSpec((1, d), lambda i: (0, 0))

    def modspec(k):
        return pl.BlockSpec((None, 1, d), lambda i: (i // per_seq, 0, k))

    in_specs = [row, row, modspec(gate_idx), vec, vec]
    args = [x2, m2, mod3, ln_g.reshape(1, d), ln_b.reshape(1, d)]
    out_specs = [row]
    out_shape = [jax.ShapeDtypeStruct((t, d), F32)]
    if with_h:
        in_specs += [modspec(h_idx[0]), modspec(h_idx[1])]
        args += [mod3, mod3]
        out_specs.append(row)
        out_shape.append(jax.ShapeDtypeStruct((t, d), BF16))
    est = 12 * _nbytes((tm, d), F32)
    outs = pl.pallas_call(
        functools.partial(_res_ln_body, alpha=alpha, with_h=with_h),
        grid=(t // tm,),
        in_specs=in_specs,
        out_specs=out_specs,
        out_shape=out_shape,
        compiler_params=_params(("parallel",), est),
    )(*args)
    return outs if with_h else outs[0]


def _conv3(cur_ref, prev_ref, next_ref, w_ref, b_ref, first, last, halo):
    x = cur_ref[...].astype(F32)
    tm = x.shape[0]
    row = lax.broadcasted_iota(jnp.int32, x.shape, 0)
    pv = prev_ref[halo - 1:halo, :].astype(F32) * jnp.where(first, 0.0, 1.0)
    nx = next_ref[0:1, :].astype(F32) * jnp.where(last, 0.0, 1.0)
    x_prev = jnp.where(row == 0, pv, pltpu.roll(x, 1, 0))
    x_next = jnp.where(row == tm - 1, nx, pltpu.roll(x, tm - 1, 0))
    w = w_ref[...]
    return x_prev * w[0:1, :] + x * w[1:2, :] + x_next * w[2:3, :] + b_ref[...]


def _conv_body(*refs, per_seq, glu, halo):
    i = pl.program_id(1)
    first = (i % per_seq) == 0
    last = (i % per_seq) == per_seq - 1
    if glu:
        ca, pa, na, wa, ba, cb, pb, nb, wb, bb, o_ref = refs
        ua = _conv3(ca, pa, na, wa, ba, first, last, halo)
        ub = _conv3(cb, pb, nb, wb, bb, first, last, halo)
        o_ref[...] = (_silu(ua) * ub).astype(o_ref.dtype)
    else:
        ca, pa, na, wa, ba, o_ref = refs
        o_ref[...] = _silu(_conv3(ca, pa, na, wa, ba, first, last, halo)).astype(o_ref.dtype)


def _dwconv_act(u, w, b, seq, glu):
    t, c_in = u.shape
    c_out = c_in // 2 if glu else c_in
    halo = V7X_BF16_SUBLANES
    tm = _pick(seq, (512, 256, 128, 64, 32, 16))
    tc = _pick(c_out, (1024, 512, 256, 128))
    per_seq = seq // tm
    hb = tm // halo
    n_halo = t // halo
    b2 = b.reshape(1, c_in)

    def group(off):
        return [pl.BlockSpec((tm, tc), lambda j, i: (i, j + off)),
                pl.BlockSpec((halo, tc), lambda j, i: (jnp.maximum(i * hb - 1, 0), j + off)),
                pl.BlockSpec((halo, tc), lambda j, i: (jnp.minimum((i + 1) * hb, n_halo - 1), j + off)),
                pl.BlockSpec((3, tc), lambda j, i: (0, j + off)),
                pl.BlockSpec((1, tc), lambda j, i: (0, j + off))]

    in_specs = group(0)
    args = [u, u, u, w, b2]
    if glu:
        in_specs += group(c_out // tc)
        args += [u, u, u, w, b2]
    est = (2 if glu else 1) * (2 * _nbytes((tm, tc), BF16) + 6 * _nbytes((tm, tc), F32)) + 2 * _nbytes((tm, tc), BF16)
    return pl.pallas_call(
        functools.partial(_conv_body, per_seq=per_seq, glu=glu, halo=halo),
        grid=(c_out // tc, t // tm),
        in_specs=in_specs,
        out_specs=pl.BlockSpec((tm, tc), lambda j, i: (i, j)),
        out_shape=jax.ShapeDtypeStruct((t, c_out), BF16),
        compiler_params=_params(("parallel", "parallel"), est),
    )(*args)


def _split3_dot(tri, a):
    a1 = a.astype(BF16)
    r1 = a - a1.astype(F32)
    a2 = r1.astype(BF16)
    a3 = (r1 - a2.astype(F32)).astype(BF16)
    dot = functools.partial(jnp.dot, preferred_element_type=F32)
    return dot(tri, a1) + dot(tri, a2) + dot(tri, a3)


def _dt_body(raw_ref, bias_ref, alog_ref, dt_ref, cum_ref, cumt_ref, *, heads, groups):
    raw = raw_ref[...]
    v = raw + bias_ref[...]
    dt = jnp.maximum(v, 0.0) + jnp.log(1.0 + jnp.exp(-jnp.abs(v)))
    a = dt * (-jnp.exp(alog_ref[...]))
    ln = raw.shape[0]
    r = lax.broadcasted_iota(jnp.int32, (ln, ln), 0)
    s = lax.broadcasted_iota(jnp.int32, (ln, ln), 1)
    lower = jnp.where(r >= s, 1.0, 0.0).astype(BF16)
    upper = jnp.where(r <= s, 1.0, 0.0).astype(BF16)
    cum = jnp.concatenate([_split3_dot(lower, a[:, :heads]), _split3_dot(upper, a[:, heads:])], axis=1)
    cum_t = cum.T
    hg = heads // groups
    for d in range(2):
        for g in range(groups):
            lo = d * heads + g * hg
            dt_ref[d, g] = dt[:, lo:lo + hg]
            cum_ref[d, g] = cum[:, lo:lo + hg]
            cumt_ref[d, g, 0] = cum_t[lo:lo + hg, :]


def _ssd_steps(dt_raw, dt_bias, a_log, groups):
    t, h2 = dt_raw.shape
    heads = h2 // 2
    hg = heads // groups
    ln = SSM_CHUNK
    nct = t // ln
    col = jax.ShapeDtypeStruct((2, groups, t, hg), F32)
    return pl.pallas_call(
        functools.partial(_dt_body, heads=heads, groups=groups),
        grid=(nct,),
        in_specs=[pl.BlockSpec((ln, h2), lambda c: (c, 0)),
                  pl.BlockSpec((1, h2), lambda c: (0, 0)),
                  pl.BlockSpec((1, h2), lambda c: (0, 0))],
        out_specs=[pl.BlockSpec((2, groups, ln, hg), lambda c: (0, 0, c, 0)),
                   pl.BlockSpec((2, groups, ln, hg), lambda c: (0, 0, c, 0)),
                   pl.BlockSpec((2, groups, 1, hg, ln), lambda c: (0, 0, c, 0, 0))],
        out_shape=[col, col, jax.ShapeDtypeStruct((2, groups, nct, hg, ln), F32)],
        compiler_params=_params(("parallel",), 0),
    )(dt_raw, dt_bias.reshape(1, h2), a_log.reshape(1, h2))


def _ssd_body(x_ref, b_ref, c_ref, dt_ref, cum_ref, cumt_ref, y_ref,
              h_ref, ch_ref, xte_ref, dec_ref, *, hg, p):
    d = pl.program_id(1)
    c = pl.program_id(3)

    @pl.when(c == 0)
    def _():
        h_ref[...] = jnp.zeros_like(h_ref)

    bm = b_ref[...]
    cm = c_ref[...]
    ln = bm.shape[0]
    cb = lax.dot_general(cm, bm, (((1,), (1,)), ((), ())), preferred_element_type=F32)
    r = lax.broadcasted_iota(jnp.int32, (ln, ln), 0)
    s = lax.broadcasted_iota(jnp.int32, (ln, ln), 1)
    mask = jnp.where(d == 0, r - s, s - r) >= 0
    ch_ref[...] = jnp.dot(cm, h_ref[...].astype(BF16), preferred_element_type=F32)

    cum = cum_ref[...]
    dt = dt_ref[...]
    cum_t = cumt_ref[...]
    width = 2 * p
    low = lax.broadcasted_iota(jnp.int32, (ln, width), 1) < p

    for i in range(hg // 2):
        ja, jb = 2 * i, 2 * i + 1
        sl = slice(i * width, (i + 1) * width)
        cum_a = jnp.broadcast_to(cum[:, ja:ja + 1], (ln, width))
        cum_b = jnp.broadcast_to(cum[:, jb:jb + 1], (ln, width))
        cum_e = jnp.where(low, cum_a, cum_b)
        dt_e = jnp.where(low, jnp.broadcast_to(dt[:, ja:ja + 1], (ln, width)),
                         jnp.broadcast_to(dt[:, jb:jb + 1], (ln, width)))
        xdt = x_ref[:, sl].astype(F32) * dt_e
        m_a = (cb * jnp.exp(jnp.where(mask, cum_a - cum_t[ja:ja + 1, :], -jnp.inf))).astype(BF16)
        m_b = (cb * jnp.exp(jnp.where(mask, cum_b - cum_t[jb:jb + 1, :], -jnp.inf))).astype(BF16)
        x_a = jnp.where(low, xdt, 0.0).astype(BF16)
        x_b = jnp.where(low, 0.0, xdt).astype(BF16)
        y = (jnp.dot(m_a, x_a, preferred_element_type=F32)
             + jnp.dot(m_b, x_b, preferred_element_type=F32)
             + jnp.exp(cum_e) * ch_ref[:, sl])
        y_ref[:, sl] = y.astype(y_ref.dtype)
        end = jnp.where(d == 0, cum_e[ln - 1:ln, :], cum_e[0:1, :])
        xte_ref[:, sl] = (xdt * jnp.exp(end - cum_e)).astype(BF16)
        dec_ref[:, sl] = jnp.exp(end)

    bm_t = bm.astype(F32).T.astype(BF16)
    h_ref[...] = h_ref[...] * dec_ref[...] + jnp.dot(bm_t, xte_ref[...], preferred_element_type=F32)


def _ssd_scan(xbc_act, dt, cum, cum_t, bsz, seq, d_inner, groups, y_dtype):
    t = xbc_act.shape[0]
    n = SSM_STATE
    ln = SSM_CHUNK
    nc = seq // ln
    hg = dt.shape[3]
    gw = d_inner // groups
    p = gw // hg
    assert 2 * p == V7X_LANES and hg % 2 == 0
    b_off = d_inner // n
    c_off = b_off + groups

    def rb(b, d, c):
        return b * nc + c + d * (nc - 1 - 2 * c)

    est = 4 * _nbytes((ln, gw), F32) + 3 * _nbytes((n, gw), F32) + 16 * _nbytes((ln, ln), F32)
    return pl.pallas_call(
        functools.partial(_ssd_body, hg=hg, p=p),
        grid=(bsz, 2, groups, nc),
        in_specs=[pl.BlockSpec((ln, gw), lambda b, d, g, c: (rb(b, d, c), g)),
                  pl.BlockSpec((ln, n), lambda b, d, g, c: (rb(b, d, c), b_off + g)),
                  pl.BlockSpec((ln, n), lambda b, d, g, c: (rb(b, d, c), c_off + g)),
                  pl.BlockSpec((None, None, ln, hg), lambda b, d, g, c: (d, g, rb(b, d, c), 0)),
                  pl.BlockSpec((None, None, ln, hg), lambda b, d, g, c: (d, g, rb(b, d, c), 0)),
                  pl.BlockSpec((None, None, None, hg, ln), lambda b, d, g, c: (d, g, rb(b, d, c), 0, 0))],
        out_specs=pl.BlockSpec((None, ln, gw), lambda b, d, g, c: (d, rb(b, d, c), g)),
        out_shape=jax.ShapeDtypeStruct((2, t, d_inner), y_dtype),
        scratch_shapes=[pltpu.VMEM((n, gw), F32), pltpu.VMEM((ln, gw), F32),
                        pltpu.VMEM((ln, gw), BF16), pltpu.VMEM((1, gw), F32)],
        compiler_params=_params(("parallel", "parallel", "parallel", "arbitrary"), est),
    )(xbc_act, xbc_act, xbc_act, dt, cum, cum_t)


def _gated_norm_body(y_ref, x_ref, z_ref, dsk_ref, w_ref, o_ref):
    y = y_ref[0].astype(F32) + y_ref[1].astype(F32) + dsk_ref[...] * x_ref[...].astype(F32)
    yg = y * _silu(z_ref[...].astype(F32))
    yg = yg * lax.rsqrt(jnp.mean(yg * yg, axis=-1, keepdims=True) + RMS_EPS)
    o_ref[...] = (yg * w_ref[...]).astype(o_ref.dtype)


def _gated_norm(y2, xbc_act, z, d_skip_e, norm_w, seq, groups):
    t, d_inner = z.shape
    gw = d_inner // groups
    tm = _pick(seq, (512, 256, 128, 64, 32, 16))
    est = 12 * _nbytes((tm, gw), F32)
    return pl.pallas_call(
        _gated_norm_body,
        grid=(t // tm, groups),
        in_specs=[pl.BlockSpec((2, tm, gw), lambda i, g: (0, i, g)),
                  pl.BlockSpec((tm, gw), lambda i, g: (i, g)),
                  pl.BlockSpec((tm, gw), lambda i, g: (i, g)),
                  pl.BlockSpec((1, gw), lambda i, g: (0, g)),
                  pl.BlockSpec((1, gw), lambda i, g: (0, g))],
        out_specs=pl.BlockSpec((tm, gw), lambda i, g: (i, g)),
        out_shape=jax.ShapeDtypeStruct((t, d_inner), BF16),
        compiler_params=_params(("parallel", "parallel"), est),
    )(y2, xbc_act, z, d_skip_e.reshape(1, d_inner), norm_w.reshape(1, d_inner))


def _rope_tables(seq, dh):
    half = dh // 2
    quarter = half // 2
    rows = seq // GRID_W
    row = jnp.repeat(jnp.arange(rows, dtype=jnp.int32), GRID_W).astype(F32)
    col = jnp.tile(jnp.arange(GRID_W, dtype=jnp.int32), rows).astype(F32)
    inv = 1.0 / (ROPE_THETA ** (jnp.arange(quarter, dtype=F32) / quarter))
    ang_r = row[:, None] * inv[None, :]
    ang_c = col[:, None] * inv[None, :]
    cos = jnp.concatenate([jnp.cos(ang_r), jnp.cos(ang_r), jnp.cos(ang_c), jnp.cos(ang_c)], axis=-1)
    sin = jnp.concatenate([-jnp.sin(ang_r), jnp.sin(ang_r), -jnp.sin(ang_c), jnp.sin(ang_c)], axis=-1)
    return cos, sin


def _norm_rope_epilogue(accs, extras, *, dh, scale):
    acc = accs[0]
    w, cos, sin = extras
    quarter = dh // 4
    lane = lax.broadcasted_iota(jnp.int32, cos.shape, 1)
    first_quarter = (lane % (2 * quarter)) < quarter
    outs = []
    for hd in range(acc.shape[1] // dh):
        v = acc[:, hd * dh:(hd + 1) * dh]
        v = v * lax.rsqrt(jnp.mean(v * v, axis=-1, keepdims=True) + RMS_EPS) * w
        partner = jnp.where(first_quarter, pltpu.roll(v, dh - quarter, 1), pltpu.roll(v, quarter, 1))
        outs.append((v * cos + partner * sin) * scale)
    return [jnp.concatenate(outs, axis=1)]


def _proj_norm_rope(h, w, norm_w, cos, sin, seq, dh, scale, tm):
    n = w.shape[1]
    tn = _pick(n, (512, 256, 128))
    per_seq = seq // tm
    specs = [pl.BlockSpec((1, dh), lambda i, j: (0, 0)),
             pl.BlockSpec((tm, dh), lambda i, j: (i % per_seq, 0)),
             pl.BlockSpec((tm, dh), lambda i, j: (i % per_seq, 0))]
    (out,) = _fused_matmul([h], [w], [norm_w.reshape(1, dh), cos, sin], specs,
                           functools.partial(_norm_rope_epilogue, dh=dh, scale=scale), [BF16], tm, tn)
    return out


def _attn_body(q_ref, k_ref, v_ref, o_ref, *, dh):
    k = k_ref[...]
    v = v_ref[...]
    for g in range(q_ref.shape[1] // dh):
        sl = slice(g * dh, (g + 1) * dh)
        s = lax.dot_general(q_ref[:, sl], k, (((1,), (1,)), ((), ())), preferred_element_type=F32)
        m = jnp.max(s, axis=-1, keepdims=True)
        e = jnp.exp(s - m)
        den = jnp.sum(e, axis=-1, keepdims=True)
        o = jnp.dot(e.astype(BF16), v, preferred_element_type=F32)
        o_ref[:, sl] = (o / den).astype(o_ref.dtype)


def _attention(q, k, v, bsz, seq, dh):
    t, q_cols = q.shape
    kv_heads = k.shape[1] // dh
    gq = q_cols // kv_heads
    tq = _pick(seq, (256, 128, 64, 32, 16))
    nq = seq // tq
    est = 4 * _nbytes((seq, dh), BF16) + 4 * _nbytes((tq, gq), BF16) + 4 * _nbytes((tq, seq), F32)
    return pl.pallas_call(
        functools.partial(_attn_body, dh=dh),
        grid=(bsz, kv_heads, nq),
        in_specs=[pl.BlockSpec((tq, gq), lambda b, j, i: (b * nq + i, j)),
                  pl.BlockSpec((seq, dh), lambda b, j, i: (b, j)),
                  pl.BlockSpec((seq, dh), lambda b, j, i: (b, j))],
        out_specs=pl.BlockSpec((tq, gq), lambda b, j, i: (b * nq + i, j)),
        out_shape=jax.ShapeDtypeStruct((t, q_cols), BF16),
        compiler_params=_params(("parallel", "parallel", "arbitrary"), est),
    )(q, k, v)


def _plain(accs, extras):
    return [accs[0]]


def _gate_epilogue(accs, extras):
    return [_sigmoid(accs[0] + extras[0])]


def _mix_epilogue(accs, extras):
    return [extras[0].astype(F32) * accs[0] + extras[1].astype(F32) * accs[1]]


def kernel(x, c, w_ada, b_ada, w_in, ssm_conv_w, ssm_conv_b, ssm_a_log, ssm_dt_bias, ssm_d, ssm_norm_w,
           q_norm_w, k_norm_w, w_ssm_proj, w_attn_proj, w_gate, b_gate, w_out, ln1_g, ln1_b, w_up,
           ffn_conv_w, ffn_conv_b, w_down, ln2_g, ln2_b):
    bsz, seq, d = x.shape
    depth = w_ada.shape[0]
    t = bsz * seq
    d_inner = w_ssm_proj.shape[1]
    heads = ssm_a_log.shape[2]
    conv_dim = ssm_conv_w.shape[2]
    groups = (conv_dim - d_inner) // (2 * SSM_STATE)
    q_cols = w_attn_proj.shape[1]
    dh = q_norm_w.shape[1]
    kv_cols = (w_in.shape[2] - d_inner - conv_dim - 2 * heads - q_cols) // 2
    ffn = w_down.shape[1]
    alpha = (2.0 * depth) ** 0.25
    o_xbc = d_inner
    o_dt = o_xbc + conv_dim
    o_q = o_dt + 2 * heads
    o_k = o_q + q_cols
    o_v = o_k + kv_cols

    tm = _pick(seq, (1024, 512, 256, 128))
    cos, sin = _rope_tables(seq, dh)
    x2 = x.reshape(t, d)

    for layer in range(depth):
        wi = w_in[layer]
        w_z = wi[:, :o_xbc].astype(BF16)
        w_xbc = wi[:, o_xbc:o_dt].astype(BF16)
        w_dt = wi[:, o_dt:o_q].astype(BF16)
        w_q = wi[:, o_q:o_k].astype(BF16)
        w_k = wi[:, o_k:o_v].astype(BF16)
        w_v = wi[:, o_v:].astype(BF16)

        mod3 = _ada_mod(c, w_ada[layer], b_ada[layer]).reshape(bsz, 1, 6 * d)
        h = _ln_modulate(x2, mod3, seq, 0, 1)

        def proj(w, dtype, tn_cands=(1024, 512, 256, 128)):
            return _fused_matmul([h], [w], [], [], _plain, [dtype], tm, _pick(w.shape[1], tn_cands))[0]

        z = proj(w_z, BF16)
        xbc = proj(w_xbc, BF16)
        dt_raw = proj(w_dt, F32)
        v = proj(w_v, BF16)
        q = _proj_norm_rope(h, w_q, q_norm_w[layer], cos, sin, seq, dh, dh ** -0.5, tm)
        k = _proj_norm_rope(h, w_k, k_norm_w[layer], cos, sin, seq, dh, 1.0, tm)

        xbc_act = _dwconv_act(xbc, ssm_conv_w[layer], ssm_conv_b[layer], seq, glu=False)
        dt, cum, cum_t = _ssd_steps(dt_raw, ssm_dt_bias[layer], ssm_a_log[layer], groups)
        y2 = _ssd_scan(xbc_act, dt, cum, cum_t, bsz, seq, d_inner, groups, F32)
        d_skip_e = jnp.repeat(ssm_d[layer], d_inner // heads)
        y_ssm = _gated_norm(y2, xbc_act, z, d_skip_e, ssm_norm_w[layer], seq, groups)

        y_attn = _attention(q, k, v, bsz, seq, dh)

        tn_d = _pick(d, (512, 256, 128))
        wg = w_gate[layer].astype(BF16)
        bg = b_gate[layer].reshape(1, 2 * d)
        (gates,) = _fused_matmul([h], [wg], [bg], [pl.BlockSpec((1, tn_d), lambda i, j: (0, j))],
                                 _gate_epilogue, [BF16], tm, tn_d)
        nb = d // tn_d
        w_sp = w_ssm_proj[layer].astype(BF16)
        w_ap = w_attn_proj[layer].astype(BF16)
        tm_mix = _fit_tm(tm, tn_d, [y_ssm, y_attn], [w_sp, w_ap], 2, [BF16])
        (mixed_pre,) = _fused_matmul(
            [y_ssm, y_attn], [w_sp, w_ap], [gates, gates],
            [pl.BlockSpec((tm_mix, tn_d), lambda i, j: (i, j)),
             pl.BlockSpec((tm_mix, tn_d), lambda i, j: (i, j + nb))],
            _mix_epilogue, [BF16], tm_mix, tn_d)
        mixed = _fused_matmul([mixed_pre], [w_out[layer].astype(BF16)], [], [], _plain, [F32], tm, tn_d)[0]
        x2, h2 = _residual_ln(x2, mixed, mod3, 2, ln1_g[layer], ln1_b[layer], seq, alpha, h_idx=(3, 4))

        w_up_b = w_up[layer].astype(BF16)
        u = _fused_matmul([h2], [w_up_b], [], [], _plain, [BF16], tm,
                          _pick(2 * ffn, (1024, 512, 256, 128)))[0]
        act = _dwconv_act(u, ffn_conv_w[layer], ffn_conv_b[layer], seq, glu=True)
        w_dn = w_down[layer].astype(BF16)
        tm_dn = _fit_tm(tm, tn_d, [act], [w_dn], 0, [F32])
        f = _fused_matmul([act], [w_dn], [], [], _plain, [F32], tm_dn, tn_d)[0]
        x2 = _residual_ln(x2, f, mod3, 5, ln2_g[layer], ln2_b[layer], seq, alpha)

    return x2.reshape(bsz, seq, d)
```

```python
import functools

import jax
import jax.numpy as jnp
from jax import lax
from jax.experimental import pallas as pl
from jax.experimental.pallas import tpu as pltpu

F32 = jnp.float32
BF16 = jnp.bfloat16

GRID_W = 64
SSM_STATE = 128
SSM_CHUNK = 128
ROPE_THETA = 10000.0
LN_EPS = 1e-5
RMS_EPS = 1e-6
LOG2_E = 1.4426950408889634

V7X_LANES = 128
V7X_MXU_DIM = 256
V7X_F32_SUBLANES = 8
V7X_BF16_SUBLANES = 16
V7X_VMEM_BYTES = 64 * 1024 * 1024
VMEM_CEILING = V7X_VMEM_BYTES - 6 * 1024 * 1024


def _pick(n, cands):
    for c in cands:
        if n % c == 0:
            return c
    return n


def _params(sem, est_bytes):
    limit = int(min(max(est_bytes, 32 * 1024 * 1024), VMEM_CEILING))
    return pltpu.CompilerParams(dimension_semantics=sem, vmem_limit_bytes=limit)


def _nbytes(shape, dtype):
    n = 1
    for s in shape:
        n *= s
    return n * jnp.dtype(dtype).itemsize


def _sigmoid(v):
    return 1.0 / (1.0 + jnp.exp(-v))


def _silu(v):
    return v * _sigmoid(v)


def _layer_norm(v):
    mu = jnp.mean(v, axis=-1, keepdims=True)
    vc = v - mu
    var = jnp.mean(vc * vc, axis=-1, keepdims=True)
    return vc * lax.rsqrt(var + LN_EPS)


def _mm_body(*refs, n_pairs, n_extras, prologue, epilogue):
    a_refs = refs[:n_pairs]
    b_refs = refs[n_pairs:2 * n_pairs]
    e_refs = refs[2 * n_pairs:2 * n_pairs + n_extras]
    o_refs = refs[2 * n_pairs + n_extras:]
    accs = []
    for a_ref, b_ref in zip(a_refs, b_refs):
        a = a_ref[...]
        if prologue is not None:
            a = prologue(a)
        accs.append(jnp.dot(a, b_ref[...].astype(BF16), preferred_element_type=F32))
    outs = epilogue(accs, [e[...] for e in e_refs])
    for o_ref, v in zip(o_refs, outs):
        o_ref[...] = v.astype(o_ref.dtype)


def _mm_vmem(tm, tn, a_list, b_list, n_extras, out_dtypes, has_prologue):
    est = sum(2 * _nbytes((tm, a.shape[1]), a.dtype) for a in a_list)
    est += sum(2 * _nbytes((b.shape[0], tn), b.dtype) for b in b_list)
    est += n_extras * 2 * _nbytes((tm, tn), F32)
    est += sum(2 * _nbytes((tm, tn), dt) for dt in out_dtypes)
    est += (len(a_list) + 2) * _nbytes((tm, tn), F32)
    if has_prologue or any(b.dtype != BF16 for b in b_list):
        est += max(_nbytes((b.shape[0], tn), BF16) for b in b_list) + _nbytes((tm, a_list[0].shape[1]), F32)
    return est


def _fit_tm(tm, tn, a_list, b_list, n_extras, out_dtypes):
    while tm > V7X_BF16_SUBLANES and _mm_vmem(tm, tn, a_list, b_list, n_extras, out_dtypes, False) > VMEM_CEILING:
        tm //= 2
    return tm


def _fused_matmul(a_list, b_list, extras, extra_specs, epilogue, out_dtypes, tm, tn, prologue=None):
    m = a_list[0].shape[0]
    n = b_list[0].shape[1]
    assert m % tm == 0 and n % tn == 0
    in_specs = ([pl.BlockSpec((tm, a.shape[1]), lambda i, j: (i, 0)) for a in a_list]
                + [pl.BlockSpec((b.shape[0], tn), lambda i, j: (0, j)) for b in b_list]
                + list(extra_specs))
    out_specs = [pl.BlockSpec((tm, tn), lambda i, j: (i, j)) for _ in out_dtypes]
    est = _mm_vmem(tm, tn, a_list, b_list, len(extras), out_dtypes, prologue is not None)
    body = functools.partial(_mm_body, n_pairs=len(a_list), n_extras=len(extras),
                             prologue=prologue, epilogue=epilogue)
    outs = pl.pallas_call(
        body,
        grid=(m // tm, n // tn),
        in_specs=in_specs,
        out_specs=out_specs,
        out_shape=[jax.ShapeDtypeStruct((m, n), dt) for dt in out_dtypes],
        compiler_params=_params(("parallel", "arbitrary"), est),
    )(*a_list, *b_list, *extras)
    return outs


def _ada_mod(c, w_ada, b_ada):
    bsz, d = c.shape
    rows = 8
    c_pad = jnp.zeros((rows, d), F32).at[:bsz].set(c)
    n = w_ada.shape[1]
    tn = _pick(n, (512, 256, 128))

    def prologue(a):
        return _silu(a).astype(BF16)

    def epilogue(accs, extras):
        return [accs[0] + extras[0]]

    (mod,) = _fused_matmul([c_pad], [w_ada], [b_ada.reshape(1, n)],
                           [pl.BlockSpec((1, tn), lambda i, j: (0, j))],
                           epilogue, [F32], rows, tn, prologue=prologue)
    return mod[:bsz]


def _ln_mod_body(x_ref, sh_ref, sc_ref, o_ref):
    y = _layer_norm(x_ref[...])
    o_ref[...] = (y * (1.0 + sc_ref[...]) + sh_ref[...]).astype(o_ref.dtype)


def _ln_modulate(x2, mod3, seq, shift_idx, scale_idx):
    t, d = x2.shape
    tm = _pick(seq, (256, 128, 64, 32, 16))
    per_seq = seq // tm
    est = 2 * _nbytes((tm, d), F32) + 2 * _nbytes((tm, d), BF16) + 4 * _nbytes((tm, d), F32)
    return pl.pallas_call(
        _ln_mod_body,
        grid=(t // tm,),
        in_specs=[pl.BlockSpec((tm, d), lambda i: (i, 0)),
                  pl.BlockSpec((None, 1, d), lambda i: (i // per_seq, 0, shift_idx)),
                  pl.BlockSpec((None, 1, d), lambda i: (i // per_seq, 0, scale_idx))],
        out_specs=pl.BlockSpec((tm, d), lambda i: (i, 0)),
        out_shape=jax.ShapeDtypeStruct((t, d), BF16),
        compiler_params=_params(("parallel",), est),
    )(x2, mod3, mod3)


def _res_ln_body(*refs, alpha, with_h):
    if with_h:
        x_ref, m_ref, gate_ref, g_ref, b_ref, sh_ref, sc_ref, o_ref, h_ref = refs
    else:
        x_ref, m_ref, gate_ref, g_ref, b_ref, o_ref = refs
    v = alpha * x_ref[...] + gate_ref[...] * m_ref[...].astype(F32)
    xn = _layer_norm(v) * g_ref[...] + b_ref[...]
    o_ref[...] = xn
    if with_h:
        h_ref[...] = (_layer_norm(xn) * (1.0 + sc_ref[...]) + sh_ref[...]).astype(h_ref.dtype)


def _residual_ln(x2, m2, mod3, gate_idx, ln_g, ln_b, seq, alpha, h_idx=None):
    t, d = x2.shape
    tm = _pick(seq, (256, 128, 64, 32, 16))
    per_seq = seq // tm
    with_h = h_idx is not None
    row = pl.BlockSpec((tm, d), lambda i: (i, 0))
    vec = pl.BlockSpec((1, d), lambda i: (0, 0))

    def modspec(k):
        return pl.BlockSpec((None, 1, d), lambda i: (i // per_seq, 0, k))

    in_specs = [row, row, modspec(gate_idx), vec, vec]
    args = [x2, m2, mod3, ln_g.reshape(1, d), ln_b.reshape(1, d)]
    out_specs = [row]
    out_shape = [jax.ShapeDtypeStruct((t, d), F32)]
    if with_h:
        in_specs += [modspec(h_idx[0]), modspec(h_idx[1])]
        args += [mod3, mod3]
        out_specs.append(row)
        out_shape.append(jax.ShapeDtypeStruct((t, d), BF16))
    est = 12 * _nbytes((tm, d), F32)
    outs = pl.pallas_call(
        functools.partial(_res_ln_body, alpha=alpha, with_h=with_h),
        grid=(t // tm,),
        in_specs=in_specs,
        out_specs=out_specs,
        out_shape=out_shape,
        compiler_params=_params(("parallel",), est),
    )(*args)
    return outs if with_h else outs[0]


def _shift_matrix(blk):
    i = jnp.arange(blk)[:, None]
    j = jnp.arange(blk)[None, :]
    return jnp.concatenate([j == i - 1, j == i + 1], axis=1).astype(BF16)


def _conv3(cur_ref, prev_ref, next_ref, w_ref, b_ref, shift, first, last, halo, rows):
    tm = cur_ref.shape[0]
    lo, hi = rows.start, rows.stop
    blk = hi - lo
    sub = V7X_F32_SUBLANES
    w = w_ref[...]
    w_side = w.astype(BF16)
    xb = cur_ref[lo:hi, :]
    side = jnp.concatenate([xb * w_side[0:1, :], xb * w_side[2:3, :]], axis=0)
    out = (jnp.dot(shift, side, preferred_element_type=F32)
           + (xb.astype(F32) * w[1:2, :] + b_ref[...]))
    if lo == 0:
        before = prev_ref[halo - 1:halo, :].astype(F32) * jnp.where(first, 0.0, 1.0)
    else:
        before = cur_ref[lo - 1:lo, :].astype(F32)
    if hi == tm:
        after = next_ref[0:1, :].astype(F32) * jnp.where(last, 0.0, 1.0)
    else:
        after = cur_ref[hi:hi + 1, :].astype(F32)
    row = lax.broadcasted_iota(jnp.int32, (sub, xb.shape[1]), 0)
    head = out[0:sub] + jnp.where(row == 0, before * w[0:1, :], 0.0)
    tail = out[blk - sub:blk] + jnp.where(row == sub - 1, after * w[2:3, :], 0.0)
    return jnp.concatenate([head, out[sub:blk - sub], tail], axis=0)


def _conv_body(*refs, per_seq, glu, halo, blk):
    i = pl.program_id(1)
    first = (i % per_seq) == 0
    last = (i % per_seq) == per_seq - 1
    shift = refs[0][...]
    o_ref = refs[-1]
    tm = o_ref.shape[0]
    for lo in range(0, tm, blk):
        rows = slice(lo, lo + blk)
        if glu:
            ca, pa, na, wa, ba, cb, pb, nb, wb, bb = refs[1:-1]
            ua = _conv3(ca, pa, na, wa, ba, shift, first, last, halo, rows)
            ub = _conv3(cb, pb, nb, wb, bb, shift, first, last, halo, rows)
            o_ref[rows, :] = (_silu(ua) * ub).astype(o_ref.dtype)
        else:
            ca, pa, na, wa, ba = refs[1:-1]
            o_ref[rows, :] = _silu(_conv3(ca, pa, na, wa, ba, shift, first, last, halo, rows)).astype(o_ref.dtype)


def _dwconv_act(u, w, b, seq, glu):
    t, c_in = u.shape
    c_out = c_in // 2 if glu else c_in
    halo = V7X_BF16_SUBLANES
    tm = _pick(seq, (2048, 1024, 512, 256, 128, 64, 32, 16))
    tc = _pick(c_out, (512, 256, 128))
    blk = min(tm, V7X_MXU_DIM)
    per_seq = seq // tm
    hb = tm // halo
    n_halo = t // halo
    b2 = b.reshape(1, c_in)

    def group(off):
        return [pl.BlockSpec((tm, tc), lambda j, i: (i, j + off)),
                pl.BlockSpec((halo, tc), lambda j, i: (jnp.maximum(i * hb - 1, 0), j + off)),
                pl.BlockSpec((halo, tc), lambda j, i: (jnp.minimum((i + 1) * hb, n_halo - 1), j + off)),
                pl.BlockSpec((3, tc), lambda j, i: (0, j + off)),
                pl.BlockSpec((1, tc), lambda j, i: (0, j + off))]

    in_specs = [pl.BlockSpec((blk, 2 * blk), lambda j, i: (0, 0))] + group(0)
    args = [_shift_matrix(blk), u, u, u, w, b2]
    if glu:
        in_specs += group(c_out // tc)
        args += [u, u, u, w, b2]
    est = ((2 if glu else 1) * (2 * _nbytes((tm, tc), BF16) + 8 * _nbytes((blk, tc), F32))
           + 2 * _nbytes((tm, tc), BF16))
    return pl.pallas_call(
        functools.partial(_conv_body, per_seq=per_seq, glu=glu, halo=halo, blk=blk),
        grid=(c_out // tc, t // tm),
        in_specs=in_specs,
        out_specs=pl.BlockSpec((tm, tc), lambda j, i: (i, j)),
        out_shape=jax.ShapeDtypeStruct((t, c_out), BF16),
        compiler_params=_params(("parallel", "parallel"), est),
    )(*args)


def _split3_dot(tri, a):
    a1 = a.astype(BF16)
    r1 = a - a1.astype(F32)
    a2 = r1.astype(BF16)
    a3 = (r1 - a2.astype(F32)).astype(BF16)
    dot = functools.partial(jnp.dot, preferred_element_type=F32)
    return dot(tri, a1) + dot(tri, a2) + dot(tri, a3)


def _dt_body(raw_ref, bias_ref, alog_ref, cum_ref, src_ref, *, heads, groups):
    raw = raw_ref[...]
    v = raw + bias_ref[...]
    dt = jnp.maximum(v, 0.0) + jnp.log(1.0 + jnp.exp(-jnp.abs(v)))
    a = dt * (-jnp.exp(alog_ref[...]))
    ln = raw.shape[0]
    r = lax.broadcasted_iota(jnp.int32, (ln, ln), 0)
    s = lax.broadcasted_iota(jnp.int32, (ln, ln), 1)
    lower = jnp.where(r >= s, 1.0, 0.0).astype(BF16)
    upper = jnp.where(r <= s, 1.0, 0.0).astype(BF16)
    cum = jnp.concatenate([_split3_dot(lower, a[:, :heads]), _split3_dot(upper, a[:, heads:])], axis=1)
    cum2 = cum * LOG2_E
    src_t = (cum2 - jnp.log(dt) * LOG2_E).T
    hg = heads // groups
    for d in range(2):
        for g in range(groups):
            lo = d * heads + g * hg
            cum_ref[d, g] = cum2[:, lo:lo + hg]
            src_ref[d, g, 0] = src_t[lo:lo + hg, :]


def _ssd_steps(dt_raw, dt_bias, a_log, groups):
    t, h2 = dt_raw.shape
    heads = h2 // 2
    hg = heads // groups
    ln = SSM_CHUNK
    nct = t // ln
    return pl.pallas_call(
        functools.partial(_dt_body, heads=heads, groups=groups),
        grid=(nct,),
        in_specs=[pl.BlockSpec((ln, h2), lambda c: (c, 0)),
                  pl.BlockSpec((1, h2), lambda c: (0, 0)),
                  pl.BlockSpec((1, h2), lambda c: (0, 0))],
        out_specs=[pl.BlockSpec((2, groups, ln, hg), lambda c: (0, 0, c, 0)),
                   pl.BlockSpec((2, groups, 1, hg, ln), lambda c: (0, 0, c, 0, 0))],
        out_shape=[jax.ShapeDtypeStruct((2, groups, t, hg), F32),
                   jax.ShapeDtypeStruct((2, groups, nct, hg, ln), F32)],
        compiler_params=_params(("parallel",), 0),
    )(dt_raw, dt_bias.reshape(1, h2), a_log.reshape(1, h2))


def _ssd_body(*refs, forward, groups, hg, p):
    if forward:
        x_ref, b_ref, c_ref, cum_ref, src_ref, yb_ref, z_ref, dsk_ref, w_ref, o_ref, h_ref, ch_ref = refs
    else:
        x_ref, b_ref, c_ref, cum_ref, src_ref, o_ref, h_ref, ch_ref = refs

    @pl.when(pl.program_id(1) == 0)
    def _():
        h_ref[...] = jnp.zeros_like(h_ref)

    ln = x_ref.shape[0]
    n = SSM_STATE
    width = 2 * p
    gw = hg * p
    r = lax.broadcasted_iota(jnp.int32, (ln, ln), 0)
    s = lax.broadcasted_iota(jnp.int32, (ln, ln), 1)
    mask = (r >= s) if forward else (r <= s)
    end_row = ln - 1 if forward else 0
    low = lax.broadcasted_iota(jnp.int32, (ln, width), 1) < p
    zero = jnp.zeros((ln, width), x_ref.dtype)

    def group(g, carry):
        gs = pl.ds(pl.multiple_of(g * gw, gw), gw)
        ns = pl.ds(pl.multiple_of(g * n, n), n)
        bm = b_ref[:, ns]
        cm = c_ref[:, ns]
        cb = lax.dot_general(cm, bm, (((1,), (1,)), ((), ())), preferred_element_type=F32)
        ch_ref[...] = jnp.dot(cm, h_ref[g].astype(BF16), preferred_element_type=F32)
        bm_t = bm.astype(F32).T

        for i in range(hg // 2):
            ja, jb = 2 * i, 2 * i + 1
            sl = slice(i * width, (i + 1) * width)
            xs = pl.ds(pl.multiple_of(g * gw + i * width, width), width)
            cum_a = jnp.broadcast_to(cum_ref[g, :, ja:ja + 1], (ln, ln))
            cum_b = jnp.broadcast_to(cum_ref[g, :, jb:jb + 1], (ln, ln))
            src_a = src_ref[g, ja:ja + 1, :]
            src_b = src_ref[g, jb:jb + 1, :]
            m_a = (cb * jnp.exp2(jnp.where(mask, cum_a - src_a, -jnp.inf))).astype(BF16)
            m_b = (cb * jnp.exp2(jnp.where(mask, cum_b - src_b, -jnp.inf))).astype(BF16)
            xp = x_ref[:, xs]
            x_ab = jnp.concatenate([jnp.where(low, xp, zero), jnp.where(low, zero, xp)], axis=0)
            cum_e = jnp.where(low, cum_a, cum_b)
            y = (jnp.dot(jnp.concatenate([m_a, m_b], axis=1), x_ab, preferred_element_type=F32)
                 + jnp.exp2(cum_e) * ch_ref[:, sl])
            if forward:
                ch_ref[:, sl] = y
            else:
                o_ref[:, xs] = y.astype(o_ref.dtype)
            end_a = cum_ref[g, end_row:end_row + 1, ja:ja + 1]
            end_b = cum_ref[g, end_row:end_row + 1, jb:jb + 1]
            bt_ab = jnp.concatenate([(bm_t * jnp.exp2(end_a - src_a)).astype(BF16),
                                     (bm_t * jnp.exp2(end_b - src_b)).astype(BF16)], axis=1)
            h_ref[g, :, sl] = (h_ref[g, :, sl] * jnp.exp2(cum_e[end_row:end_row + 1, :])
                               + jnp.dot(bt_ab, x_ab, preferred_element_type=F32))

        if forward:
            y = ch_ref[...] + yb_ref[:, gs].astype(F32) + dsk_ref[:, gs] * x_ref[:, gs].astype(F32)
            yg = y * _silu(z_ref[:, gs].astype(F32))
            yg = yg * lax.rsqrt(jnp.mean(yg * yg, axis=-1, keepdims=True) + RMS_EPS)
            o_ref[:, gs] = (yg * w_ref[:, gs]).astype(o_ref.dtype)
        return carry

    lax.fori_loop(0, groups, group, 0, unroll=2)


def _ssd_scan(xbc_act, cum, src, bsz, seq, d_inner, groups, forward, epilogue_inputs=()):
    t = xbc_act.shape[0]
    n = SSM_STATE
    ln = SSM_CHUNK
    nc = seq // ln
    hg = cum.shape[3]
    heads = hg * groups
    p = d_inner // heads
    gw = hg * p
    gn = groups * n
    assert 2 * p == V7X_LANES and ln == V7X_LANES and hg % 2 == 0 and d_inner % gn == 0
    d = 0 if forward else 1

    def rb(b, c):
        return b * nc + (c if forward else nc - 1 - c)

    rows = pl.BlockSpec((ln, d_inner), lambda b, c: (rb(b, c), 0))
    vec = pl.BlockSpec((1, d_inner), lambda b, c: (0, 0))
    in_specs = [rows,
                pl.BlockSpec((ln, gn), lambda b, c: (rb(b, c), d_inner // gn)),
                pl.BlockSpec((ln, gn), lambda b, c: (rb(b, c), d_inner // gn + 1)),
                pl.BlockSpec((None, groups, ln, hg), lambda b, c: (d, 0, rb(b, c), 0)),
                pl.BlockSpec((None, groups, None, hg, ln), lambda b, c: (d, 0, rb(b, c), 0, 0))]
    args = [xbc_act, xbc_act, xbc_act, cum, src]
    if forward:
        y_bwd, z, d_skip_e, norm_w = epilogue_inputs
        in_specs += [rows, rows, vec, vec]
        args += [y_bwd, z, d_skip_e.reshape(1, d_inner), norm_w.reshape(1, d_inner)]
    est = (2 * (4 if forward else 2) * _nbytes((ln, d_inner), BF16) + 4 * _nbytes((ln, gn), BF16)
           + _nbytes((n, d_inner), F32) + 10 * _nbytes((ln, gw), F32) + 16 * _nbytes((ln, ln), F32))
    return pl.pallas_call(
        functools.partial(_ssd_body, forward=forward, groups=groups, hg=hg, p=p),
        grid=(bsz, nc),
        in_specs=in_specs,
        out_specs=rows,
        out_shape=jax.ShapeDtypeStruct((t, d_inner), BF16),
        scratch_shapes=[pltpu.VMEM((groups, n, gw), F32), pltpu.VMEM((ln, gw), F32)],
        compiler_params=_params(("parallel", "arbitrary"), est),
    )(*args)


def _rope_tables(seq, dh):
    half = dh // 2
    quarter = half // 2
    rows = seq // GRID_W
    row = jnp.repeat(jnp.arange(rows, dtype=jnp.int32), GRID_W).astype(F32)
    col = jnp.tile(jnp.arange(GRID_W, dtype=jnp.int32), rows).astype(F32)
    inv = 1.0 / (ROPE_THETA ** (jnp.arange(quarter, dtype=F32) / quarter))
    ang_r = row[:, None] * inv[None, :]
    ang_c = col[:, None] * inv[None, :]
    cos = jnp.concatenate([jnp.cos(ang_r), jnp.cos(ang_c), jnp.cos(ang_r), jnp.cos(ang_c)], axis=-1)
    sin = jnp.concatenate([-jnp.sin(ang_r), -jnp.sin(ang_c), jnp.sin(ang_r), jnp.sin(ang_c)], axis=-1)
    return cos, sin


def _pair_major(a, dh):
    lead = a.shape[:-1]
    nh = a.shape[-1] // dh
    a = a.reshape(lead + (nh, 2, 2, dh // 4))
    return jnp.swapaxes(a, -3, -2).reshape(lead + (nh * dh,))


def _norm_rope_epilogue(accs, extras, *, dh, scale):
    acc = accs[0]
    w, cos, sin = extras
    w = w * scale
    ones = jnp.ones((dh, dh), BF16)
    outs = []
    for hd in range(acc.shape[1] // dh):
        v = acc[:, hd * dh:(hd + 1) * dh]
        ss = jnp.dot((v * v).astype(BF16), ones, preferred_element_type=F32)
        v = v * lax.rsqrt(ss * (1.0 / dh) + RMS_EPS) * w
        outs.append(v * cos + pltpu.roll(v, dh // 2, 1) * sin)
    return [jnp.concatenate(outs, axis=1)]


def _proj_norm_rope(h, w, norm_w, cos, sin, seq, dh, scale, tm):
    n = w.shape[1]
    tn = _pick(n, (512, 256, 128))
    per_seq = seq // tm
    specs = [pl.BlockSpec((1, dh), lambda i, j: (0, 0)),
             pl.BlockSpec((tm, dh), lambda i, j: (i % per_seq, 0)),
             pl.BlockSpec((tm, dh), lambda i, j: (i % per_seq, 0))]
    (out,) = _fused_matmul([h], [_pair_major(w, dh)], [_pair_major(norm_w, dh).reshape(1, dh), cos, sin], specs,
                           functools.partial(_norm_rope_epilogue, dh=dh, scale=scale), [BF16], tm, tn)
    return out


def _attn_body(q_ref, k_ref, v_ref, o_ref, *, dh):
    k = k_ref[...]
    v = v_ref[...]
    for g in range(q_ref.shape[1] // dh):
        sl = slice(g * dh, (g + 1) * dh)
        s = lax.dot_general(q_ref[:, sl], k, (((1,), (1,)), ((), ())), preferred_element_type=F32)
        e = jnp.exp2(s - jnp.max(s, axis=-1, keepdims=True))
        den = jnp.sum(e, axis=-1, keepdims=True)
        o = jnp.dot(e.astype(BF16), v, preferred_element_type=F32)
        o_ref[:, sl] = (o / den).astype(o_ref.dtype)


def _attention(q, k, v, bsz, seq, dh):
    t, q_cols = q.shape
    kv_heads = k.shape[1] // dh
    gq = q_cols // kv_heads
    tq = _pick(seq, (256, 128, 64, 32, 16))
    nq = seq // tq
    est = (4 * _nbytes((seq, dh), BF16) + 4 * _nbytes((tq, gq), BF16)
           + 4 * _nbytes((tq, seq), F32) + 2 * _nbytes((tq, seq), BF16))
    return pl.pallas_call(
        functools.partial(_attn_body, dh=dh),
        grid=(bsz, kv_heads, nq),
        in_specs=[pl.BlockSpec((tq, gq), lambda b, j, i: (b * nq + i, j)),
                  pl.BlockSpec((seq, dh), lambda b, j, i: (b, j)),
                  pl.BlockSpec((seq, dh), lambda b, j, i: (b, j))],
        out_specs=pl.BlockSpec((tq, gq), lambda b, j, i: (b * nq + i, j)),
        out_shape=jax.ShapeDtypeStruct((t, q_cols), BF16),
        compiler_params=_params(("parallel", "parallel", "arbitrary"), est),
    )(q, k, v)


def _plain(accs, extras):
    return [accs[0]]


def _gate_epilogue(accs, extras):
    return [_sigmoid(accs[0] + extras[0])]


def _mix_epilogue(accs, extras):
    return [extras[0].astype(F32) * accs[0] + extras[1].astype(F32) * accs[1]]


def kernel(x, c, w_ada, b_ada, w_in, ssm_conv_w, ssm_conv_b, ssm_a_log, ssm_dt_bias, ssm_d, ssm_norm_w,
           q_norm_w, k_norm_w, w_ssm_proj, w_attn_proj, w_gate, b_gate, w_out, ln1_g, ln1_b, w_up,
           ffn_conv_w, ffn_conv_b, w_down, ln2_g, ln2_b):
    bsz, seq, d = x.shape
    depth = w_ada.shape[0]
    t = bsz * seq
    d_inner = w_ssm_proj.shape[1]
    heads = ssm_a_log.shape[2]
    conv_dim = ssm_conv_w.shape[2]
    groups = (conv_dim - d_inner) // (2 * SSM_STATE)
    q_cols = w_attn_proj.shape[1]
    dh = q_norm_w.shape[1]
    kv_cols = (w_in.shape[2] - d_inner - conv_dim - 2 * heads - q_cols) // 2
    ffn = w_down.shape[1]
    alpha = (2.0 * depth) ** 0.25
    o_xbc = d_inner
    o_dt = o_xbc + conv_dim
    o_q = o_dt + 2 * heads
    o_k = o_q + q_cols
    o_v = o_k + kv_cols

    tm = _pick(seq, (1024, 512, 256, 128))
    cos, sin = _rope_tables(seq, dh)
    x2 = x.reshape(t, d)

    for layer in range(depth):
        wi = w_in[layer]
        w_z = wi[:, :o_xbc].astype(BF16)
        w_xbc = wi[:, o_xbc:o_dt].astype(BF16)
        w_dt = wi[:, o_dt:o_q].astype(BF16)
        w_q = wi[:, o_q:o_k].astype(BF16)
        w_k = wi[:, o_k:o_v].astype(BF16)
        w_v = wi[:, o_v:].astype(BF16)

        mod3 = _ada_mod(c, w_ada[layer], b_ada[layer]).reshape(bsz, 1, 6 * d)
        h = _ln_modulate(x2, mod3, seq, 0, 1)

        def proj(w, dtype, tn_cands=(1024, 512, 256, 128)):
            return _fused_matmul([h], [w], [], [], _plain, [dtype], tm, _pick(w.shape[1], tn_cands))[0]

        z = proj(w_z, BF16)
        xbc = proj(w_xbc, BF16)
        dt_raw = proj(w_dt, F32)
        v = proj(w_v, BF16)
        q = _proj_norm_rope(h, w_q, q_norm_w[layer], cos, sin, seq, dh, dh ** -0.5 * LOG2_E, tm)
        k = _proj_norm_rope(h, w_k, k_norm_w[layer], cos, sin, seq, dh, 1.0, tm)

        xbc_act = _dwconv_act(xbc, ssm_conv_w[layer], ssm_conv_b[layer], seq, glu=False)
        cum, src = _ssd_steps(dt_raw, ssm_dt_bias[layer], ssm_a_log[layer], groups)
        y_bwd = _ssd_scan(xbc_act, cum, src, bsz, seq, d_inner, groups, forward=False)
        d_skip_e = jnp.repeat(ssm_d[layer], d_inner // heads)
        y_ssm = _ssd_scan(xbc_act, cum, src, bsz, seq, d_inner, groups, forward=True,
                          epilogue_inputs=(y_bwd, z, d_skip_e, ssm_norm_w[layer]))

        y_attn = _attention(q, k, v, bsz, seq, dh)

        tn_d = _pick(d, (512, 256, 128))
        tn_wide = _pick(d, (1024, 512, 256, 128))
        wg = w_gate[layer].astype(BF16)
        bg = b_gate[layer].reshape(1, 2 * d)
        (gates,) = _fused_matmul([h], [wg], [bg], [pl.BlockSpec((1, tn_wide), lambda i, j: (0, j))],
                                 _gate_epilogue, [BF16], tm, tn_wide)
        nb = d // tn_d
        w_sp = w_ssm_proj[layer].astype(BF16)
        w_ap = w_attn_proj[layer].astype(BF16)
        tm_mix = _fit_tm(tm, tn_d, [y_ssm, y_attn], [w_sp, w_ap], 2, [BF16])
        (mixed_pre,) = _fused_matmul(
            [y_ssm, y_attn], [w_sp, w_ap], [gates, gates],
            [pl.BlockSpec((tm_mix, tn_d), lambda i, j: (i, j)),
             pl.BlockSpec((tm_mix, tn_d), lambda i, j: (i, j + nb))],
            _mix_epilogue, [BF16], tm_mix, tn_d)
        mixed = _fused_matmul([mixed_pre], [w_out[layer].astype(BF16)], [], [], _plain, [BF16], tm, tn_wide)[0]
        x2, h2 = _residual_ln(x2, mixed, mod3, 2, ln1_g[layer], ln1_b[layer], seq, alpha, h_idx=(3, 4))

        w_up_b = w_up[layer].astype(BF16)
        tn_up = _pick(2 * ffn, (1024, 512, 256, 128))
        tm_up = _fit_tm(_pick(seq, (2048, 1024, 512, 256, 128)), tn_up, [h2], [w_up_b], 0, [BF16])
        u = _fused_matmul([h2], [w_up_b], [], [], _plain, [BF16], tm_up, tn_up)[0]
        act = _dwconv_act(u, ffn_conv_w[layer], ffn_conv_b[layer], seq, glu=True)
        w_dn = w_down[layer].astype(BF16)
        tm_dn = _fit_tm(tm, tn_d, [act], [w_dn], 0, [BF16])
        f = _fused_matmul([act], [w_dn], [], [], _plain, [BF16], tm_dn, tn_d)[0]
        x2 = _residual_ln(x2, f, mod3, 5, ln2_g[layer], ln2_b[layer], seq, alpha)

    return x2.reshape(bsz, seq, d)
```

```python
import functools

import jax
import jax.numpy as jnp
from jax import lax
from jax.experimental import pallas as pl
from jax.experimental.pallas import tpu as pltpu

F32 = jnp.float32
BF16 = jnp.bfloat16

GRID_W = 64
SSM_STATE = 128
SSM_CHUNK = 128
ROPE_THETA = 10000.0
LN_EPS = 1e-5
RMS_EPS = 1e-6
LOG2_E = 1.4426950408889634

V7X_LANES = 128
CONV_ROW_BLOCK = 64
V7X_F32_SUBLANES = 8
V7X_BF16_SUBLANES = 16
V7X_VMEM_BYTES = 64 * 1024 * 1024
VMEM_CEILING = V7X_VMEM_BYTES - 6 * 1024 * 1024


def _pick(n, cands):
    for c in cands:
        if n % c == 0:
            return c
    return n


def _params(sem, est_bytes):
    limit = int(min(max(est_bytes, 32 * 1024 * 1024), VMEM_CEILING))
    return pltpu.CompilerParams(dimension_semantics=sem, vmem_limit_bytes=limit)


def _nbytes(shape, dtype):
    n = 1
    for s in shape:
        n *= s
    return n * jnp.dtype(dtype).itemsize


def _sigmoid(v):
    return 1.0 / (1.0 + jnp.exp(-v))


def _silu(v):
    return v * _sigmoid(v)


def _layer_norm(v):
    mu = jnp.mean(v, axis=-1, keepdims=True)
    vc = v - mu
    var = jnp.mean(vc * vc, axis=-1, keepdims=True)
    return vc * lax.rsqrt(var + LN_EPS)


def _mm_body(*refs, n_pairs, n_extras, prologue, epilogue):
    a_refs = refs[:n_pairs]
    b_refs = refs[n_pairs:2 * n_pairs]
    e_refs = refs[2 * n_pairs:2 * n_pairs + n_extras]
    o_refs = refs[2 * n_pairs + n_extras:]
    accs = []
    for a_ref, b_ref in zip(a_refs, b_refs):
        a = a_ref[...]
        if prologue is not None:
            a = prologue(a)
        accs.append(jnp.dot(a, b_ref[...].astype(BF16), preferred_element_type=F32))
    outs = epilogue(accs, [e[...] for e in e_refs])
    for o_ref, v in zip(o_refs, outs):
        o_ref[...] = v.astype(o_ref.dtype)


def _mm_vmem(tm, tn, a_list, b_list, n_extras, out_dtypes, has_prologue):
    est = sum(2 * _nbytes((tm, a.shape[1]), a.dtype) for a in a_list)
    est += sum(2 * _nbytes((b.shape[0], tn), b.dtype) for b in b_list)
    est += n_extras * 2 * _nbytes((tm, tn), F32)
    est += sum(2 * _nbytes((tm, tn), dt) for dt in out_dtypes)
    est += (len(a_list) + 2) * _nbytes((tm, tn), F32)
    if has_prologue or any(b.dtype != BF16 for b in b_list):
        est += max(_nbytes((b.shape[0], tn), BF16) for b in b_list) + _nbytes((tm, a_list[0].shape[1]), F32)
    return est


def _fit_tm(tm, tn, a_list, b_list, n_extras, out_dtypes):
    while tm > V7X_BF16_SUBLANES and _mm_vmem(tm, tn, a_list, b_list, n_extras, out_dtypes, False) > VMEM_CEILING:
        tm //= 2
    return tm


def _fused_matmul(a_list, b_list, extras, extra_specs, epilogue, out_dtypes, tm, tn, prologue=None):
    m = a_list[0].shape[0]
    n = b_list[0].shape[1]
    assert m % tm == 0 and n % tn == 0
    in_specs = ([pl.BlockSpec((tm, a.shape[1]), lambda i, j: (i, 0)) for a in a_list]
                + [pl.BlockSpec((b.shape[0], tn), lambda i, j: (0, j)) for b in b_list]
                + list(extra_specs))
    out_specs = [pl.BlockSpec((tm, tn), lambda i, j: (i, j)) for _ in out_dtypes]
    est = _mm_vmem(tm, tn, a_list, b_list, len(extras), out_dtypes, prologue is not None)
    body = functools.partial(_mm_body, n_pairs=len(a_list), n_extras=len(extras),
                             prologue=prologue, epilogue=epilogue)
    outs = pl.pallas_call(
        body,
        grid=(m // tm, n // tn),
        in_specs=in_specs,
        out_specs=out_specs,
        out_shape=[jax.ShapeDtypeStruct((m, n), dt) for dt in out_dtypes],
        compiler_params=_params(("parallel", "arbitrary"), est),
    )(*a_list, *b_list, *extras)
    return outs


def _ada_mod(c, w_ada, b_ada):
    bsz, d = c.shape
    rows = 8
    c_pad = jnp.zeros((rows, d), F32).at[:bsz].set(c)
    n = w_ada.shape[1]
    tn = _pick(n, (512, 256, 128))

    def prologue(a):
        return _silu(a).astype(BF16)

    def epilogue(accs, extras):
        return [accs[0] + extras[0]]

    (mod,) = _fused_matmul([c_pad], [w_ada], [b_ada.reshape(1, n)],
                           [pl.BlockSpec((1, tn), lambda i, j: (0, j))],
                           epilogue, [F32], rows, tn, prologue=prologue)
    return mod[:bsz]


def _ln_mod_body(x_ref, sh_ref, sc_ref, o_ref):
    y = _layer_norm(x_ref[...])
    o_ref[...] = (y * (1.0 + sc_ref[...]) + sh_ref[...]).astype(o_ref.dtype)


def _ln_modulate(x2, mod3, seq, shift_idx, scale_idx):
    t, d = x2.shape
    tm = _pick(seq, (256, 128, 64, 32, 16))
    per_seq = seq // tm
    est = 2 * _nbytes((tm, d), F32) + 2 * _nbytes((tm, d), BF16) + 4 * _nbytes((tm, d), F32)
    return pl.pallas_call(
        _ln_mod_body,
        grid=(t // tm,),
        in_specs=[pl.BlockSpec((tm, d), lambda i: (i, 0)),
                  pl.BlockSpec((None, 1, d), lambda i: (i // per_seq, 0, shift_idx)),
                  pl.BlockSpec((None, 1, d), lambda i: (i // per_seq, 0, scale_idx))],
        out_specs=pl.BlockSpec((tm, d), lambda i: (i, 0)),
        out_shape=jax.ShapeDtypeStruct((t, d), BF16),
        compiler_params=_params(("parallel",), est),
    )(x2, mod3, mod3)


def _res_ln_body(*refs, alpha, with_h):
    if with_h:
        x_ref, m_ref, gate_ref, g_ref, b_ref, sh_ref, sc_ref, o_ref, h_ref = refs
    else:
        x_ref, m_ref, gate_ref, g_ref, b_ref, o_ref = refs
    v = alpha * x_ref[...] + gate_ref[...] * m_ref[...].astype(F32)
    xn = _layer_norm(v) * g_ref[...] + b_ref[...]
    o_ref[...] = xn
    if with_h:
        h_ref[...] = (_layer_norm(xn) * (1.0 + sc_ref[...]) + sh_ref[...]).astype(h_ref.dtype)


def _residual_ln(x2, m2, mod3, gate_idx, ln_g, ln_b, seq, alpha, h_idx=None):
    t, d = x2.shape
    tm = _pick(seq, (256, 128, 64, 32, 16))
    per_seq = seq // tm
    with_h = h_idx is not None
    row = pl.BlockSpec((tm, d), lambda i: (i, 0))
    vec = pl.BlockSpec((1, d), lambda i: (0, 0))

    def modspec(k):
        return pl.BlockSpec((None, 1, d), lambda i: (i // per_seq, 0, k))

    in_specs = [row, row, modspec(gate_idx), vec, vec]
    args = [x2, m2, mod3, ln_g.reshape(1, d), ln_b.reshape(1, d)]
    out_specs = [row]
    out_shape = [jax.ShapeDtypeStruct((t, d), F32)]
    if with_h:
        in_specs += [modspec(h_idx[0]), modspec(h_idx[1])]
        args += [mod3, mod3]
        out_specs.append(row)
        out_shape.append(jax.ShapeDtypeStruct((t, d), BF16))
    est = 12 * _nbytes((tm, d), F32)
    outs = pl.pallas_call(
        functools.partial(_res_ln_body, alpha=alpha, with_h=with_h),
        grid=(t // tm,),
        in_specs=in_specs,
        out_specs=out_specs,
        out_shape=out_shape,
        compiler_params=_params(("parallel",), est),
    )(*args)
    return outs if with_h else outs[0]


def _conv3_rows(u_ref, w_ref, b_ref, rows, halo):
    lo, hi = rows.start + halo, rows.stop + halo
    blk = hi - lo
    w = w_ref[...]
    x = u_ref[lo:hi, :].astype(F32)
    x_prev = jnp.concatenate([u_ref[lo - 1:lo, :].astype(F32), x[:blk - 1]], axis=0)
    x_next = jnp.concatenate([x[1:], u_ref[hi:hi + 1, :].astype(F32)], axis=0)
    return x_prev * w[0:1, :] + x * w[1:2, :] + x_next * w[2:3, :] + b_ref[...]


def _proj_conv_body(*refs, glu, n_col, n_row, per_seq, blk, halo):
    if glu:
        (a_ref, ap_ref, an_ref, wa_ref, wb_ref, cwa_ref, cba_ref, cwb_ref, cbb_ref,
         o_ref, aext_ref, ua0_ref, ub0_ref, ua1_ref, ub1_ref) = refs
        u_slots = ((ua0_ref, ub0_ref), (ua1_ref, ub1_ref))
        w_refs = (wa_ref, wb_ref)
        conv_refs = ((cwa_ref, cba_ref), (cwb_ref, cbb_ref))
    else:
        a_ref, ap_ref, an_ref, wa_ref, cwa_ref, cba_ref, o_ref, aext_ref, ua0_ref, ua1_ref = refs
        u_slots = ((ua0_ref,), (ua1_ref,))
        w_refs = (wa_ref,)
        conv_refs = ((cwa_ref, cba_ref),)
    s = pl.program_id(0)
    j = s % n_col
    i = jnp.minimum(s // n_col, n_row - 1)
    tm = a_ref.shape[0]

    @pl.when(j == 0)
    def _():
        first = (i % per_seq) == 0
        last = (i % per_seq) == per_seq - 1
        aext_ref[0:halo, :] = jnp.where(first, jnp.zeros_like(ap_ref), ap_ref[...])
        aext_ref[halo:halo + tm, :] = a_ref[...]
        aext_ref[halo + tm:, :] = jnp.where(last, jnp.zeros_like(an_ref), an_ref[...])

    @pl.when(s == 0)
    def _():
        for u_ref in u_slots[1]:
            u_ref[...] = jnp.zeros_like(u_ref)

    def step(dst, src):
        n_blk = tm // blk
        parts = len(w_refs)
        for part in range(parts):
            for lo in range(part * n_blk // parts * blk, (part + 1) * n_blk // parts * blk, blk):
                rows = slice(lo, lo + blk)
                vals = [_conv3_rows(u_ref, cw_ref, cb_ref, rows, halo)
                        for u_ref, (cw_ref, cb_ref) in zip(u_slots[src], conv_refs)]
                act = _silu(vals[0]) * vals[1] if glu else _silu(vals[0])
                o_ref[rows, :] = act.astype(o_ref.dtype)
            u_slots[dst][part][...] = jnp.dot(aext_ref[...], w_refs[part][...],
                                              preferred_element_type=F32).astype(BF16)

    pl.when(s % 2 == 0)(lambda: step(0, 1))
    pl.when(s % 2 == 1)(lambda: step(1, 0))


def _proj_conv_act(a, w, conv_w, conv_b, seq, glu):
    t, k = a.shape
    c_in = w.shape[1]
    c_out = c_in // 2 if glu else c_in
    halo = V7X_BF16_SUBLANES
    tm = _pick(seq, (1024, 512, 256, 128, 64, 32, 16))
    tn = _pick(c_out, (256, 128)) if glu else _pick(c_out, (512, 256, 128))
    blk = min(tm, CONV_ROW_BLOCK)
    n_row, n_col, per_seq = t // tm, c_out // tn, seq // tm
    hb = tm // halo
    n_halo = t // halo
    b2 = conv_b.reshape(1, c_in)

    def row_tile(s):
        return jnp.minimum(s // n_col, n_row - 1)

    def conv_tile(s):
        return jnp.maximum(s - 1, 0)

    in_specs = [pl.BlockSpec((tm, k), lambda s: (row_tile(s), 0)),
                pl.BlockSpec((halo, k), lambda s: (jnp.maximum(row_tile(s) * hb - 1, 0), 0)),
                pl.BlockSpec((halo, k), lambda s: (jnp.minimum((row_tile(s) + 1) * hb, n_halo - 1), 0)),
                pl.BlockSpec((k, tn), lambda s: (0, s % n_col))]
    args = [a, a, a, w]
    if glu:
        in_specs.append(pl.BlockSpec((k, tn), lambda s: (0, n_col + s % n_col)))
        args.append(w)
    for off in ((0, n_col) if glu else (0,)):
        in_specs += [pl.BlockSpec((3, tn), lambda s, off=off: (0, off + conv_tile(s) % n_col)),
                     pl.BlockSpec((1, tn), lambda s, off=off: (0, off + conv_tile(s) % n_col))]
        args += [conv_w, b2]
    parts = 2 if glu else 1
    est = (3 * _nbytes((tm + 2 * halo, k), BF16) + parts * 2 * _nbytes((k, tn), BF16)
           + parts * (2 * _nbytes((tm + 2 * halo, tn), BF16) + _nbytes((tm + 2 * halo, tn), F32)
                      + 8 * _nbytes((blk, tn), F32))
           + 2 * _nbytes((tm, tn), BF16))
    return pl.pallas_call(
        functools.partial(_proj_conv_body, glu=glu, n_col=n_col, n_row=n_row, per_seq=per_seq,
                          blk=blk, halo=halo),
        grid=(n_row * n_col + 1,),
        in_specs=in_specs,
        out_specs=pl.BlockSpec((tm, tn), lambda s: (conv_tile(s) // n_col, conv_tile(s) % n_col)),
        out_shape=jax.ShapeDtypeStruct((t, c_out), BF16),
        scratch_shapes=([pltpu.VMEM((tm + 2 * halo, k), BF16)]
                        + [pltpu.VMEM((tm + 2 * halo, tn), BF16)] * (2 * parts)),
        compiler_params=_params(("arbitrary",), est),
    )(*args)


def _split3_dot(tri, a):
    a1 = a.astype(BF16)
    r1 = a - a1.astype(F32)
    a2 = r1.astype(BF16)
    a3 = (r1 - a2.astype(F32)).astype(BF16)
    dot = functools.partial(jnp.dot, preferred_element_type=F32)
    return dot(tri, a1) + dot(tri, a2) + dot(tri, a3)


def _dt_body(raw_ref, bias_ref, alog_ref, cum_ref, src_ref, *, heads, groups):
    raw = raw_ref[...]
    v = raw + bias_ref[...]
    dt = jnp.maximum(v, 0.0) + jnp.log(1.0 + jnp.exp(-jnp.abs(v)))
    a = dt * (-jnp.exp(alog_ref[...]))
    ln = raw.shape[0]
    r = lax.broadcasted_iota(jnp.int32, (ln, ln), 0)
    s = lax.broadcasted_iota(jnp.int32, (ln, ln), 1)
    lower = jnp.where(r >= s, 1.0, 0.0).astype(BF16)
    upper = jnp.where(r <= s, 1.0, 0.0).astype(BF16)
    cum = jnp.concatenate([_split3_dot(lower, a[:, :heads]), _split3_dot(upper, a[:, heads:])], axis=1)
    cum2 = cum * LOG2_E
    src_t = (cum2 - jnp.log(dt) * LOG2_E).T
    hg = heads // groups
    for d in range(2):
        for g in range(groups):
            lo = d * heads + g * hg
            cum_ref[d, g] = cum2[:, lo:lo + hg]
            src_ref[d, g, 0] = src_t[lo:lo + hg, :]


def _ssd_steps(dt_raw, dt_bias, a_log, groups):
    t, h2 = dt_raw.shape
    heads = h2 // 2
    hg = heads // groups
    ln = SSM_CHUNK
    nct = t // ln
    return pl.pallas_call(
        functools.partial(_dt_body, heads=heads, groups=groups),
        grid=(nct,),
        in_specs=[pl.BlockSpec((ln, h2), lambda c: (c, 0)),
                  pl.BlockSpec((1, h2), lambda c: (0, 0)),
                  pl.BlockSpec((1, h2), lambda c: (0, 0))],
        out_specs=[pl.BlockSpec((2, groups, ln, hg), lambda c: (0, 0, c, 0)),
                   pl.BlockSpec((2, groups, 1, hg, ln), lambda c: (0, 0, c, 0, 0))],
        out_shape=[jax.ShapeDtypeStruct((2, groups, t, hg), F32),
                   jax.ShapeDtypeStruct((2, groups, nct, hg, ln), F32)],
        compiler_params=_params(("parallel",), 0),
    )(dt_raw, dt_bias.reshape(1, h2), a_log.reshape(1, h2))


def _ssd_body(*refs, forward, groups, hg, p):
    if forward:
        x_ref, b_ref, c_ref, cum_ref, src_ref, yb_ref, z_ref, dsk_ref, w_ref, o_ref, h_ref, ch_ref = refs
    else:
        x_ref, b_ref, c_ref, cum_ref, src_ref, o_ref, h_ref, ch_ref = refs

    @pl.when(pl.program_id(1) == 0)
    def _():
        h_ref[...] = jnp.zeros_like(h_ref)

    ln = x_ref.shape[0]
    n = SSM_STATE
    width = 2 * p
    gw = hg * p
    r = lax.broadcasted_iota(jnp.int32, (ln, ln), 0)
    s = lax.broadcasted_iota(jnp.int32, (ln, ln), 1)
    mask = (r >= s) if forward else (r <= s)
    end_row = ln - 1 if forward else 0
    low = lax.broadcasted_iota(jnp.int32, (ln, width), 1) < p
    zero = jnp.zeros((ln, width), x_ref.dtype)

    def group(g, carry):
        gs = pl.ds(pl.multiple_of(g * gw, gw), gw)
        ns = pl.ds(pl.multiple_of(g * n, n), n)
        bm = b_ref[:, ns]
        cm = c_ref[:, ns]
        cb = lax.dot_general(cm, bm, (((1,), (1,)), ((), ())), preferred_element_type=F32)
        ch_ref[...] = jnp.dot(cm, h_ref[g].astype(BF16), preferred_element_type=F32)
        bm_t = bm.astype(F32).T

        for i in range(hg // 2):
            ja, jb = 2 * i, 2 * i + 1
            sl = slice(i * width, (i + 1) * width)
            xs = pl.ds(pl.multiple_of(g * gw + i * width, width), width)
            cum_a = jnp.broadcast_to(cum_ref[g, :, ja:ja + 1], (ln, ln))
            cum_b = jnp.broadcast_to(cum_ref[g, :, jb:jb + 1], (ln, ln))
            src_a = src_ref[g, ja:ja + 1, :]
            src_b = src_ref[g, jb:jb + 1, :]
            m_a = (cb * jnp.exp2(jnp.where(mask, cum_a - src_a, -jnp.inf))).astype(BF16)
            m_b = (cb * jnp.exp2(jnp.where(mask, cum_b - src_b, -jnp.inf))).astype(BF16)
            xp = x_ref[:, xs]
            x_ab = jnp.concatenate([jnp.where(low, xp, zero), jnp.where(low, zero, xp)], axis=0)
            cum_e = jnp.where(low, cum_a, cum_b)
            y = (jnp.dot(jnp.concatenate([m_a, m_b], axis=1), x_ab, preferred_element_type=F32)
                 + jnp.exp2(cum_e) * ch_ref[:, sl])
            if forward:
                ch_ref[:, sl] = y
            else:
                o_ref[:, xs] = y.astype(o_ref.dtype)
            end_a = cum_ref[g, end_row:end_row + 1, ja:ja + 1]
            end_b = cum_ref[g, end_row:end_row + 1, jb:jb + 1]
            bt_ab = jnp.concatenate([(bm_t * jnp.exp2(end_a - src_a)).astype(BF16),
                                     (bm_t * jnp.exp2(end_b - src_b)).astype(BF16)], axis=1)
            h_ref[g, :, sl] = (h_ref[g, :, sl] * jnp.exp2(cum_e[end_row:end_row + 1, :])
                               + jnp.dot(bt_ab, x_ab, preferred_element_type=F32))

        if forward:
            y = ch_ref[...] + yb_ref[:, gs].astype(F32) + dsk_ref[:, gs] * x_ref[:, gs].astype(F32)
            yg = y * _silu(z_ref[:, gs].astype(F32))
            yg = yg * lax.rsqrt(jnp.mean(yg * yg, axis=-1, keepdims=True) + RMS_EPS)
            o_ref[:, gs] = (yg * w_ref[:, gs]).astype(o_ref.dtype)
        return carry

    lax.fori_loop(0, groups, group, 0, unroll=2)


def _ssd_scan(xbc_act, cum, src, bsz, seq, d_inner, groups, forward, epilogue_inputs=()):
    t = xbc_act.shape[0]
    n = SSM_STATE
    ln = SSM_CHUNK
    nc = seq // ln
    hg = cum.shape[3]
    heads = hg * groups
    p = d_inner // heads
    gw = hg * p
    gn = groups * n
    assert 2 * p == V7X_LANES and ln == V7X_LANES and hg % 2 == 0 and d_inner % gn == 0
    d = 0 if forward else 1

    def rb(b, c):
        return b * nc + (c if forward else nc - 1 - c)

    rows = pl.BlockSpec((ln, d_inner), lambda b, c: (rb(b, c), 0))
    vec = pl.BlockSpec((1, d_inner), lambda b, c: (0, 0))
    in_specs = [rows,
                pl.BlockSpec((ln, gn), lambda b, c: (rb(b, c), d_inner // gn)),
                pl.BlockSpec((ln, gn), lambda b, c: (rb(b, c), d_inner // gn + 1)),
                pl.BlockSpec((None, groups, ln, hg), lambda b, c: (d, 0, rb(b, c), 0)),
                pl.BlockSpec((None, groups, None, hg, ln), lambda b, c: (d, 0, rb(b, c), 0, 0))]
    args = [xbc_act, xbc_act, xbc_act, cum, src]
    if forward:
        y_bwd, z, d_skip_e, norm_w = epilogue_inputs
        in_specs += [rows, rows, vec, vec]
        args += [y_bwd, z, d_skip_e.reshape(1, d_inner), norm_w.reshape(1, d_inner)]
    est = (2 * (4 if forward else 2) * _nbytes((ln, d_inner), BF16) + 4 * _nbytes((ln, gn), BF16)
           + _nbytes((n, d_inner), F32) + 10 * _nbytes((ln, gw), F32) + 16 * _nbytes((ln, ln), F32))
    return pl.pallas_call(
        functools.partial(_ssd_body, forward=forward, groups=groups, hg=hg, p=p),
        grid=(bsz, nc),
        in_specs=in_specs,
        out_specs=rows,
        out_shape=jax.ShapeDtypeStruct((t, d_inner), BF16),
        scratch_shapes=[pltpu.VMEM((groups, n, gw), F32), pltpu.VMEM((ln, gw), F32)],
        compiler_params=_params(("parallel", "arbitrary"), est),
    )(*args)


def _rope_tables(seq, dh):
    half = dh // 2
    quarter = half // 2
    rows = seq // GRID_W
    row = jnp.repeat(jnp.arange(rows, dtype=jnp.int32), GRID_W).astype(F32)
    col = jnp.tile(jnp.arange(GRID_W, dtype=jnp.int32), rows).astype(F32)
    inv = 1.0 / (ROPE_THETA ** (jnp.arange(quarter, dtype=F32) / quarter))
    ang_r = row[:, None] * inv[None, :]
    ang_c = col[:, None] * inv[None, :]
    cos = jnp.concatenate([jnp.cos(ang_r), jnp.cos(ang_c), jnp.cos(ang_r), jnp.cos(ang_c)], axis=-1)
    sin = jnp.concatenate([-jnp.sin(ang_r), -jnp.sin(ang_c), jnp.sin(ang_r), jnp.sin(ang_c)], axis=-1)
    return cos, sin


def _pair_major(a, dh):
    lead = a.shape[:-1]
    nh = a.shape[-1] // dh
    a = a.reshape(lead + (nh, 2, 2, dh // 4))
    return jnp.swapaxes(a, -3, -2).reshape(lead + (nh * dh,))


def _norm_rope_epilogue(accs, extras, *, dh, scale):
    acc = accs[0]
    w, cos, sin = extras
    w = w * scale
    ones = jnp.ones((dh, dh), BF16)
    outs = []
    for hd in range(acc.shape[1] // dh):
        v = acc[:, hd * dh:(hd + 1) * dh]
        ss = jnp.dot((v * v).astype(BF16), ones, preferred_element_type=F32)
        v = v * lax.rsqrt(ss * (1.0 / dh) + RMS_EPS) * w
        outs.append(v * cos + pltpu.roll(v, dh // 2, 1) * sin)
    return [jnp.concatenate(outs, axis=1)]


def _proj_norm_rope(h, w, norm_w, cos, sin, seq, dh, scale, tm):
    n = w.shape[1]
    tn = _pick(n, (512, 256, 128))
    per_seq = seq // tm
    specs = [pl.BlockSpec((1, dh), lambda i, j: (0, 0)),
             pl.BlockSpec((tm, dh), lambda i, j: (i % per_seq, 0)),
             pl.BlockSpec((tm, dh), lambda i, j: (i % per_seq, 0))]
    (out,) = _fused_matmul([h], [_pair_major(w, dh)], [_pair_major(norm_w, dh).reshape(1, dh), cos, sin], specs,
                           functools.partial(_norm_rope_epilogue, dh=dh, scale=scale), [BF16], tm, tn)
    return out


def _attn_body(q_ref, k_ref, v_ref, o_ref, *, dh):
    k = k_ref[...]
    v = v_ref[...]
    for g in range(q_ref.shape[1] // dh):
        sl = slice(g * dh, (g + 1) * dh)
        s = lax.dot_general(q_ref[:, sl], k, (((1,), (1,)), ((), ())), preferred_element_type=F32)
        e = jnp.exp2(s - jnp.max(s, axis=-1, keepdims=True))
        den = jnp.sum(e, axis=-1, keepdims=True)
        o = jnp.dot(e.astype(BF16), v, preferred_element_type=F32)
        o_ref[:, sl] = (o / den).astype(o_ref.dtype)


def _attention(q, k, v, bsz, seq, dh):
    t, q_cols = q.shape
    kv_heads = k.shape[1] // dh
    gq = q_cols // kv_heads
    tq = _pick(seq, (256, 128, 64, 32, 16))
    nq = seq // tq
    est = (4 * _nbytes((seq, dh), BF16) + 4 * _nbytes((tq, gq), BF16)
           + 4 * _nbytes((tq, seq), F32) + 2 * _nbytes((tq, seq), BF16))
    return pl.pallas_call(
        functools.partial(_attn_body, dh=dh),
        grid=(bsz, kv_heads, nq),
        in_specs=[pl.BlockSpec((tq, gq), lambda b, j, i: (b * nq + i, j)),
                  pl.BlockSpec((seq, dh), lambda b, j, i: (b, j)),
                  pl.BlockSpec((seq, dh), lambda b, j, i: (b, j))],
        out_specs=pl.BlockSpec((tq, gq), lambda b, j, i: (b * nq + i, j)),
        out_shape=jax.ShapeDtypeStruct((t, q_cols), BF16),
        compiler_params=_params(("parallel", "parallel", "arbitrary"), est),
    )(q, k, v)


def _plain(accs, extras):
    return [accs[0]]


def _gate_epilogue(accs, extras):
    return [_sigmoid(accs[0] + extras[0])]


def _mix_epilogue(accs, extras):
    return [extras[0].astype(F32) * accs[0] + extras[1].astype(F32) * accs[1]]


def kernel(x, c, w_ada, b_ada, w_in, ssm_conv_w, ssm_conv_b, ssm_a_log, ssm_dt_bias, ssm_d, ssm_norm_w,
           q_norm_w, k_norm_w, w_ssm_proj, w_attn_proj, w_gate, b_gate, w_out, ln1_g, ln1_b, w_up,
           ffn_conv_w, ffn_conv_b, w_down, ln2_g, ln2_b):
    bsz, seq, d = x.shape
    depth = w_ada.shape[0]
    t = bsz * seq
    d_inner = w_ssm_proj.shape[1]
    heads = ssm_a_log.shape[2]
    conv_dim = ssm_conv_w.shape[2]
    groups = (conv_dim - d_inner) // (2 * SSM_STATE)
    q_cols = w_attn_proj.shape[1]
    dh = q_norm_w.shape[1]
    kv_cols = (w_in.shape[2] - d_inner - conv_dim - 2 * heads - q_cols) // 2
    ffn = w_down.shape[1]
    alpha = (2.0 * depth) ** 0.25
    o_xbc = d_inner
    o_dt = o_xbc + conv_dim
    o_q = o_dt + 2 * heads
    o_k = o_q + q_cols
    o_v = o_k + kv_cols

    tm = _pick(seq, (1024, 512, 256, 128))
    cos, sin = _rope_tables(seq, dh)
    x2 = x.reshape(t, d)

    for layer in range(depth):
        wi = w_in[layer]
        w_z = wi[:, :o_xbc].astype(BF16)
        w_xbc = wi[:, o_xbc:o_dt].astype(BF16)
        w_dt = wi[:, o_dt:o_q].astype(BF16)
        w_q = wi[:, o_q:o_k].astype(BF16)
        w_k = wi[:, o_k:o_v].astype(BF16)
        w_v = wi[:, o_v:].astype(BF16)

        mod3 = _ada_mod(c, w_ada[layer], b_ada[layer]).reshape(bsz, 1, 6 * d)
        h = _ln_modulate(x2, mod3, seq, 0, 1)

        def proj(w, dtype, tn_cands=(1024, 512, 256, 128)):
            return _fused_matmul([h], [w], [], [], _plain, [dtype], tm, _pick(w.shape[1], tn_cands))[0]

        z = proj(w_z, BF16)
        dt_raw = proj(w_dt, F32)
        v = proj(w_v, BF16)
        q = _proj_norm_rope(h, w_q, q_norm_w[layer], cos, sin, seq, dh, dh ** -0.5 * LOG2_E, tm)
        k = _proj_norm_rope(h, w_k, k_norm_w[layer], cos, sin, seq, dh, 1.0, tm)

        xbc_act = _proj_conv_act(h, w_xbc, ssm_conv_w[layer], ssm_conv_b[layer], seq, glu=False)
        cum, src = _ssd_steps(dt_raw, ssm_dt_bias[layer], ssm_a_log[layer], groups)
        y_bwd = _ssd_scan(xbc_act, cum, src, bsz, seq, d_inner, groups, forward=False)
        d_skip_e = jnp.repeat(ssm_d[layer], d_inner // heads)
        y_ssm = _ssd_scan(xbc_act, cum, src, bsz, seq, d_inner, groups, forward=True,
                          epilogue_inputs=(y_bwd, z, d_skip_e, ssm_norm_w[layer]))

        y_attn = _attention(q, k, v, bsz, seq, dh)

        tn_d = _pick(d, (512, 256, 128))
        tn_wide = _pick(d, (1024, 512, 256, 128))
        wg = w_gate[layer].astype(BF16)
        bg = b_gate[layer].reshape(1, 2 * d)
        (gates,) = _fused_matmul([h], [wg], [bg], [pl.BlockSpec((1, tn_wide), lambda i, j: (0, j))],
                                 _gate_epilogue, [BF16], tm, tn_wide)
        nb = d // tn_d
        w_sp = w_ssm_proj[layer].astype(BF16)
        w_ap = w_attn_proj[layer].astype(BF16)
        tm_mix = _fit_tm(tm, tn_d, [y_ssm, y_attn], [w_sp, w_ap], 2, [BF16])
        (mixed_pre,) = _fused_matmul(
            [y_ssm, y_attn], [w_sp, w_ap], [gates, gates],
            [pl.BlockSpec((tm_mix, tn_d), lambda i, j: (i, j)),
             pl.BlockSpec((tm_mix, tn_d), lambda i, j: (i, j + nb))],
            _mix_epilogue, [BF16], tm_mix, tn_d)
        mixed = _fused_matmul([mixed_pre], [w_out[layer].astype(BF16)], [], [], _plain, [BF16], tm, tn_wide)[0]
        x2, h2 = _residual_ln(x2, mixed, mod3, 2, ln1_g[layer], ln1_b[layer], seq, alpha, h_idx=(3, 4))

        w_up_b = w_up[layer].astype(BF16)
        act = _proj_conv_act(h2, w_up_b, ffn_conv_w[layer], ffn_conv_b[layer], seq, glu=True)
        w_dn = w_down[layer].astype(BF16)
        tm_dn = _fit_tm(tm, tn_d, [act], [w_dn], 0, [BF16])
        f = _fused_matmul([act], [w_dn], [], [], _plain, [BF16], tm_dn, tn_d)[0]
        x2 = _residual_ln(x2, f, mod3, 5, ln2_g[layer], ln2_b[layer], seq, alpha)

    return x2.reshape(bsz, seq, d)
```

```python
import functools
import math

import jax
import jax.numpy as jnp
from jax import lax
from jax.experimental import pallas as pl
from jax.experimental.pallas import tpu as pltpu

F32 = jnp.float32
BF16 = jnp.bfloat16

GRID_W = 64
SSM_STATE = 128
SSM_CHUNK = 128
ROPE_THETA = 10000.0
LN_EPS = 1e-5
RMS_EPS = 1e-6
LOG2_E = 1.4426950408889634

V7X_LANES = 128
CONV_ROW_BLOCK = 64
V7X_F32_SUBLANES = 8
V7X_BF16_SUBLANES = 16
V7X_VMEM_BYTES = 64 * 1024 * 1024
VMEM_CEILING = V7X_VMEM_BYTES - 6 * 1024 * 1024


def _pick(n, cands):
    for c in cands:
        if n % c == 0:
            return c
    return n


def _params(sem, est_bytes):
    limit = int(min(max(est_bytes, 32 * 1024 * 1024), VMEM_CEILING))
    return pltpu.CompilerParams(dimension_semantics=sem, vmem_limit_bytes=limit)


def _nbytes(shape, dtype):
    n = 1
    for s in shape:
        n *= s
    return n * jnp.dtype(dtype).itemsize


def _sigmoid(v):
    return 1.0 / (1.0 + jnp.exp(-v))


def _silu(v):
    return v * _sigmoid(v)


def _layer_norm(v):
    mu = jnp.mean(v, axis=-1, keepdims=True)
    vc = v - mu
    var = jnp.mean(vc * vc, axis=-1, keepdims=True)
    return vc * lax.rsqrt(var + LN_EPS)


def _mm_body(*refs, n_pairs, n_extras, prologue, epilogue):
    a_refs = refs[:n_pairs]
    b_refs = refs[n_pairs:2 * n_pairs]
    e_refs = refs[2 * n_pairs:2 * n_pairs + n_extras]
    o_refs = refs[2 * n_pairs + n_extras:]
    accs = []
    for a_ref, b_ref in zip(a_refs, b_refs):
        a = a_ref[...]
        if prologue is not None:
            a = prologue(a)
        accs.append(jnp.dot(a, b_ref[...].astype(BF16), preferred_element_type=F32))
    outs = epilogue(accs, [e[...] for e in e_refs])
    for o_ref, v in zip(o_refs, outs):
        o_ref[...] = v.astype(o_ref.dtype)


def _mm_vmem(tm, tn, a_list, b_list, n_extras, out_dtypes, has_prologue):
    est = sum(2 * _nbytes((tm, a.shape[1]), a.dtype) for a in a_list)
    est += sum(2 * _nbytes((b.shape[0], tn), b.dtype) for b in b_list)
    est += n_extras * 2 * _nbytes((tm, tn), F32)
    est += sum(2 * _nbytes((tm, tn), dt) for dt in out_dtypes)
    est += (len(a_list) + 2) * _nbytes((tm, tn), F32)
    if has_prologue or any(b.dtype != BF16 for b in b_list):
        est += max(_nbytes((b.shape[0], tn), BF16) for b in b_list) + _nbytes((tm, a_list[0].shape[1]), F32)
    return est


def _fit_tm(tm, tn, a_list, b_list, n_extras, out_dtypes):
    while tm > V7X_BF16_SUBLANES and _mm_vmem(tm, tn, a_list, b_list, n_extras, out_dtypes, False) > VMEM_CEILING:
        tm //= 2
    return tm


def _fused_matmul(a_list, b_list, extras, extra_specs, epilogue, out_dtypes, tm, tn, prologue=None,
                  b_cols=None):
    m = a_list[0].shape[0]
    col0, n = b_cols if b_cols is not None else (0, b_list[0].shape[1])
    assert m % tm == 0 and n % tn == 0 and col0 % tn == 0
    off = col0 // tn
    in_specs = ([pl.BlockSpec((tm, a.shape[1]), lambda i, j: (i, 0)) for a in a_list]
                + [pl.BlockSpec((b.shape[0], tn), lambda i, j: (0, j + off)) for b in b_list]
                + list(extra_specs))
    out_specs = [pl.BlockSpec((tm, tn), lambda i, j: (i, j)) for _ in out_dtypes]
    est = _mm_vmem(tm, tn, a_list, b_list, len(extras), out_dtypes, prologue is not None)
    body = functools.partial(_mm_body, n_pairs=len(a_list), n_extras=len(extras),
                             prologue=prologue, epilogue=epilogue)
    outs = pl.pallas_call(
        body,
        grid=(m // tm, n // tn),
        in_specs=in_specs,
        out_specs=out_specs,
        out_shape=[jax.ShapeDtypeStruct((m, n), dt) for dt in out_dtypes],
        compiler_params=_params(("parallel", "arbitrary"), est),
    )(*a_list, *b_list, *extras)
    return outs


def _ada_mod(c, w_ada, b_ada):
    bsz, d = c.shape
    rows = 8
    c_pad = jnp.zeros((rows, d), F32).at[:bsz].set(c)
    n = w_ada.shape[1]
    tn = _pick(n, (512, 256, 128))

    def prologue(a):
        return _silu(a).astype(BF16)

    def epilogue(accs, extras):
        return [accs[0] + extras[0]]

    (mod,) = _fused_matmul([c_pad], [w_ada], [b_ada.reshape(1, n)],
                           [pl.BlockSpec((1, tn), lambda i, j: (0, j))],
                           epilogue, [F32], rows, tn, prologue=prologue)
    return mod[:bsz]


def _ln_mod_body(x_ref, sh_ref, sc_ref, o_ref):
    y = _layer_norm(x_ref[...])
    o_ref[...] = (y * (1.0 + sc_ref[...]) + sh_ref[...]).astype(o_ref.dtype)


def _ln_modulate(x2, mod3, seq, shift_idx, scale_idx):
    t, d = x2.shape
    tm = _pick(seq, (256, 128, 64, 32, 16))
    per_seq = seq // tm
    est = 2 * _nbytes((tm, d), F32) + 2 * _nbytes((tm, d), BF16) + 4 * _nbytes((tm, d), F32)
    return pl.pallas_call(
        _ln_mod_body,
        grid=(t // tm,),
        in_specs=[pl.BlockSpec((tm, d), lambda i: (i, 0)),
                  pl.BlockSpec((None, 1, d), lambda i: (i // per_seq, 0, shift_idx)),
                  pl.BlockSpec((None, 1, d), lambda i: (i // per_seq, 0, scale_idx))],
        out_specs=pl.BlockSpec((tm, d), lambda i: (i, 0)),
        out_shape=jax.ShapeDtypeStruct((t, d), BF16),
        compiler_params=_params(("parallel",), est),
    )(x2, mod3, mod3)


def _res_ln_body(*refs, alpha, with_h):
    if with_h:
        x_ref, m_ref, gate_ref, g_ref, b_ref, sh_ref, sc_ref, o_ref, h_ref = refs
    else:
        x_ref, m_ref, gate_ref, g_ref, b_ref, o_ref = refs
    v = alpha * x_ref[...] + gate_ref[...] * m_ref[...].astype(F32)
    xn = _layer_norm(v) * g_ref[...] + b_ref[...]
    o_ref[...] = xn
    if with_h:
        h_ref[...] = (_layer_norm(xn) * (1.0 + sc_ref[...]) + sh_ref[...]).astype(h_ref.dtype)


def _residual_ln(x2, m2, mod3, gate_idx, ln_g, ln_b, seq, alpha, h_idx=None):
    t, d = x2.shape
    tm = _pick(seq, (256, 128, 64, 32, 16))
    per_seq = seq // tm
    with_h = h_idx is not None
    row = pl.BlockSpec((tm, d), lambda i: (i, 0))
    vec = pl.BlockSpec((1, d), lambda i: (0, 0))

    def modspec(k):
        return pl.BlockSpec((None, 1, d), lambda i: (i // per_seq, 0, k))

    in_specs = [row, row, modspec(gate_idx), vec, vec]
    args = [x2, m2, mod3, ln_g.reshape(1, d), ln_b.reshape(1, d)]
    out_specs = [row]
    out_shape = [jax.ShapeDtypeStruct((t, d), F32)]
    if with_h:
        in_specs += [modspec(h_idx[0]), modspec(h_idx[1])]
        args += [mod3, mod3]
        out_specs.append(row)
        out_shape.append(jax.ShapeDtypeStruct((t, d), BF16))
    est = 12 * _nbytes((tm, d), F32)
    outs = pl.pallas_call(
        functools.partial(_res_ln_body, alpha=alpha, with_h=with_h),
        grid=(t // tm,),
        in_specs=in_specs,
        out_specs=out_specs,
        out_shape=out_shape,
        compiler_params=_params(("parallel",), est),
    )(*args)
    return outs if with_h else outs[0]


def _conv3_rows(u_ref, w_ref, b_ref, rows, halo):
    lo, hi = rows.start + halo, rows.stop + halo
    blk = hi - lo
    w = w_ref[...]
    x = u_ref[lo:hi, :].astype(F32)
    x_prev = jnp.concatenate([u_ref[lo - 1:lo, :].astype(F32), x[:blk - 1]], axis=0)
    x_next = jnp.concatenate([x[1:], u_ref[hi:hi + 1, :].astype(F32)], axis=0)
    return x_prev * w[0:1, :] + x * w[1:2, :] + x_next * w[2:3, :] + b_ref[...]


def _proj_conv_body(*refs, glu, n_col, n_row, per_seq, blk, halo):
    if glu:
        (a_ref, ap_ref, an_ref, wa_ref, wb_ref, cwa_ref, cba_ref, cwb_ref, cbb_ref,
         o_ref, aext_ref, ua_prev, ub_prev, ua_next, ub_next) = refs
        u_prev, u_next = (ua_prev, ub_prev), (ua_next, ub_next)
        w_refs = (wa_ref, wb_ref)
        conv_refs = ((cwa_ref, cba_ref), (cwb_ref, cbb_ref))
    else:
        a_ref, ap_ref, an_ref, wa_ref, cwa_ref, cba_ref, o_ref, aext_ref, ua_prev, ua_next = refs
        u_prev, u_next = (ua_prev,), (ua_next,)
        w_refs = (wa_ref,)
        conv_refs = ((cwa_ref, cba_ref),)
    s = pl.program_id(0)
    j = s % n_col
    i = jnp.minimum(s // n_col, n_row - 1)
    tm = a_ref.shape[0]

    @pl.when(j == 0)
    def _():
        first = (i % per_seq) == 0
        last = (i % per_seq) == per_seq - 1
        aext_ref[0:halo, :] = jnp.where(first, jnp.zeros_like(ap_ref), ap_ref[...])
        aext_ref[halo:halo + tm, :] = a_ref[...]
        aext_ref[halo + tm:, :] = jnp.where(last, jnp.zeros_like(an_ref), an_ref[...])

    @pl.when(s == 0)
    def _():
        for u_ref in u_next:
            u_ref[...] = jnp.zeros_like(u_ref)

    for dst_ref, src_ref in zip(u_prev, u_next):
        dst_ref[...] = src_ref[...]

    @pl.when(jnp.bitwise_xor(s, i) >= 0)
    def _():
        n_blk = tm // blk
        parts = len(w_refs)
        for part in range(parts):
            for lo in range(part * n_blk // parts * blk, (part + 1) * n_blk // parts * blk, blk):
                rows = slice(lo, lo + blk)
                vals = [_conv3_rows(u_ref, cw_ref, cb_ref, rows, halo)
                        for u_ref, (cw_ref, cb_ref) in zip(u_prev, conv_refs)]
                act = _silu(vals[0]) * vals[1] if glu else _silu(vals[0])
                o_ref[rows, :] = act.astype(o_ref.dtype)
            u_next[part][...] = jnp.dot(aext_ref[...], w_refs[part][...],
                                        preferred_element_type=F32).astype(BF16)


def _proj_conv_act(a, w, conv_w, conv_b, seq, glu, col0=0):
    t, k = a.shape
    c_in = conv_w.shape[1]
    c_out = c_in // 2 if glu else c_in
    halo = V7X_BF16_SUBLANES
    tm = _pick(seq, (1024, 512, 256, 128, 64, 32, 16))
    tn = _pick(c_out, (256, 128)) if glu else _pick(c_out, (512, 256, 128))
    blk = min(tm, CONV_ROW_BLOCK)
    n_row, n_col, per_seq = t // tm, c_out // tn, seq // tm
    hb = tm // halo
    n_halo = t // halo
    b2 = conv_b.reshape(1, c_in)
    assert col0 % tn == 0
    w_off = col0 // tn

    def row_tile(s):
        return jnp.minimum(s // n_col, n_row - 1)

    def conv_tile(s):
        return jnp.maximum(s - 1, 0)

    in_specs = [pl.BlockSpec((tm, k), lambda s: (row_tile(s), 0)),
                pl.BlockSpec((halo, k), lambda s: (jnp.maximum(row_tile(s) * hb - 1, 0), 0)),
                pl.BlockSpec((halo, k), lambda s: (jnp.minimum((row_tile(s) + 1) * hb, n_halo - 1), 0)),
                pl.BlockSpec((k, tn), lambda s: (0, w_off + s % n_col))]
    args = [a, a, a, w]
    if glu:
        in_specs.append(pl.BlockSpec((k, tn), lambda s: (0, w_off + n_col + s % n_col)))
        args.append(w)
    for off in ((0, n_col) if glu else (0,)):
        in_specs += [pl.BlockSpec((3, tn), lambda s, off=off: (0, off + conv_tile(s) % n_col)),
                     pl.BlockSpec((1, tn), lambda s, off=off: (0, off + conv_tile(s) % n_col))]
        args += [conv_w, b2]
    parts = 2 if glu else 1
    est = (3 * _nbytes((tm + 2 * halo, k), BF16) + parts * 2 * _nbytes((k, tn), BF16)
           + parts * (2 * _nbytes((tm + 2 * halo, tn), BF16) + _nbytes((tm + 2 * halo, tn), F32)
                      + 8 * _nbytes((blk, tn), F32))
           + 2 * _nbytes((tm, tn), BF16))
    return pl.pallas_call(
        functools.partial(_proj_conv_body, glu=glu, n_col=n_col, n_row=n_row, per_seq=per_seq,
                          blk=blk, halo=halo),
        grid=(n_row * n_col + 1,),
        in_specs=in_specs,
        out_specs=pl.BlockSpec((tm, tn), lambda s: (conv_tile(s) // n_col, conv_tile(s) % n_col)),
        out_shape=jax.ShapeDtypeStruct((t, c_out), BF16),
        scratch_shapes=([pltpu.VMEM((tm + 2 * halo, k), BF16)]
                        + [pltpu.VMEM((tm + 2 * halo, tn), BF16)] * (2 * parts)),
        compiler_params=_params(("arbitrary",), est),
    )(*args)


def _split3_dot(tri, a):
    a1 = a.astype(BF16)
    r1 = a - a1.astype(F32)
    a2 = r1.astype(BF16)
    a3 = (r1 - a2.astype(F32)).astype(BF16)
    dot = functools.partial(jnp.dot, preferred_element_type=F32)
    return dot(tri, a1) + dot(tri, a2) + dot(tri, a3)


def _dt_body(raw_ref, bias_ref, alog_ref, cum_ref, src_ref, *, heads, groups):
    raw = raw_ref[...]
    v = raw + bias_ref[...]
    dt = jnp.maximum(v, 0.0) + jnp.log(1.0 + jnp.exp(-jnp.abs(v)))
    a = dt * (-jnp.exp(alog_ref[...]))
    ln = raw.shape[0]
    r = lax.broadcasted_iota(jnp.int32, (ln, ln), 0)
    s = lax.broadcasted_iota(jnp.int32, (ln, ln), 1)
    lower = jnp.where(r >= s, 1.0, 0.0).astype(BF16)
    upper = jnp.where(r <= s, 1.0, 0.0).astype(BF16)
    cum = jnp.concatenate([_split3_dot(lower, a[:, :heads]), _split3_dot(upper, a[:, heads:])], axis=1)
    cum2 = cum * LOG2_E
    src_t = (cum2 - jnp.log(dt) * LOG2_E).T
    hg = heads // groups
    for d in range(2):
        for g in range(groups):
            lo = d * heads + g * hg
            cum_ref[d, g] = cum2[:, lo:lo + hg]
            src_ref[d, g, 0] = src_t[lo:lo + hg, :]


def _ssd_steps(dt_raw, dt_bias, a_log, groups):
    t, h2 = dt_raw.shape
    heads = h2 // 2
    hg = heads // groups
    ln = SSM_CHUNK
    nct = t // ln
    return pl.pallas_call(
        functools.partial(_dt_body, heads=heads, groups=groups),
        grid=(nct,),
        in_specs=[pl.BlockSpec((ln, h2), lambda c: (c, 0)),
                  pl.BlockSpec((1, h2), lambda c: (0, 0)),
                  pl.BlockSpec((1, h2), lambda c: (0, 0))],
        out_specs=[pl.BlockSpec((2, groups, ln, hg), lambda c: (0, 0, c, 0)),
                   pl.BlockSpec((2, groups, 1, hg, ln), lambda c: (0, 0, c, 0, 0))],
        out_shape=[jax.ShapeDtypeStruct((2, groups, t, hg), F32),
                   jax.ShapeDtypeStruct((2, groups, nct, hg, ln), F32)],
        compiler_params=_params(("parallel",), 0),
    )(dt_raw, dt_bias.reshape(1, h2), a_log.reshape(1, h2))


def _ssd_body(*refs, forward, groups, hg, p):
    if forward:
        x_ref, b_ref, c_ref, cum_ref, src_ref, yb_ref, z_ref, dsk_ref, w_ref, o_ref, h_ref, ch_ref = refs
    else:
        x_ref, b_ref, c_ref, cum_ref, src_ref, o_ref, h_ref, ch_ref = refs

    @pl.when(pl.program_id(1) == 0)
    def _():
        h_ref[...] = jnp.zeros_like(h_ref)

    ln = x_ref.shape[0]
    n = SSM_STATE
    width = 2 * p
    gw = hg * p
    r = lax.broadcasted_iota(jnp.int32, (ln, ln), 0)
    s = lax.broadcasted_iota(jnp.int32, (ln, ln), 1)
    mask = (r >= s) if forward else (r <= s)
    end_row = ln - 1 if forward else 0
    low = lax.broadcasted_iota(jnp.int32, (ln, width), 1) < p
    zero = jnp.zeros((ln, width), x_ref.dtype)

    def group(g, carry):
        gs = pl.ds(pl.multiple_of(g * gw, gw), gw)
        ns = pl.ds(pl.multiple_of(g * n, n), n)
        bm = b_ref[:, ns]
        cm = c_ref[:, ns]
        cb = lax.dot_general(cm, bm, (((1,), (1,)), ((), ())), preferred_element_type=F32)
        ch_ref[...] = jnp.dot(cm, h_ref[g].astype(BF16), preferred_element_type=F32)
        bm_t = bm.astype(F32).T

        for i in range(hg // 2):
            ja, jb = 2 * i, 2 * i + 1
            sl = slice(i * width, (i + 1) * width)
            xs = pl.ds(pl.multiple_of(g * gw + i * width, width), width)
            cum_a = jnp.broadcast_to(cum_ref[g, :, ja:ja + 1], (ln, ln))
            cum_b = jnp.broadcast_to(cum_ref[g, :, jb:jb + 1], (ln, ln))
            src_a = src_ref[g, ja:ja + 1, :]
            src_b = src_ref[g, jb:jb + 1, :]
            m_a = (cb * jnp.exp2(jnp.where(mask, cum_a - src_a, -jnp.inf))).astype(BF16)
            m_b = (cb * jnp.exp2(jnp.where(mask, cum_b - src_b, -jnp.inf))).astype(BF16)
            xp = x_ref[:, xs]
            x_ab = jnp.concatenate([jnp.where(low, xp, zero), jnp.where(low, zero, xp)], axis=0)
            cum_e = jnp.where(low, cum_a, cum_b)
            y = (jnp.dot(jnp.concatenate([m_a, m_b], axis=1), x_ab, preferred_element_type=F32)
                 + jnp.exp2(cum_e) * ch_ref[:, sl])
            if forward:
                ch_ref[:, sl] = y
            else:
                o_ref[:, xs] = y.astype(o_ref.dtype)
            end_a = cum_ref[g, end_row:end_row + 1, ja:ja + 1]
            end_b = cum_ref[g, end_row:end_row + 1, jb:jb + 1]
            bt_ab = jnp.concatenate([(bm_t * jnp.exp2(end_a - src_a)).astype(BF16),
                                     (bm_t * jnp.exp2(end_b - src_b)).astype(BF16)], axis=1)
            h_ref[g, :, sl] = (h_ref[g, :, sl] * jnp.exp2(cum_e[end_row:end_row + 1, :])
                               + jnp.dot(bt_ab, x_ab, preferred_element_type=F32))

        if forward:
            y = ch_ref[...] + yb_ref[:, gs].astype(F32) + dsk_ref[:, gs] * x_ref[:, gs].astype(F32)
            yg = y * _silu(z_ref[:, gs].astype(F32))
            yg = yg * lax.rsqrt(jnp.mean(yg * yg, axis=-1, keepdims=True) + RMS_EPS)
            o_ref[:, gs] = (yg * w_ref[:, gs]).astype(o_ref.dtype)
        return carry

    lax.fori_loop(0, groups, group, 0, unroll=4)


def _ssd_scan(xbc_act, cum, src, bsz, seq, d_inner, groups, forward, epilogue_inputs=()):
    t = xbc_act.shape[0]
    n = SSM_STATE
    ln = SSM_CHUNK
    nc = seq // ln
    hg = cum.shape[3]
    heads = hg * groups
    p = d_inner // heads
    gw = hg * p
    gn = groups * n
    assert 2 * p == V7X_LANES and ln == V7X_LANES and hg % 2 == 0 and d_inner % gn == 0
    d = 0 if forward else 1

    def rb(b, c):
        return b * nc + (c if forward else nc - 1 - c)

    rows = pl.BlockSpec((ln, d_inner), lambda b, c: (rb(b, c), 0))
    vec = pl.BlockSpec((1, d_inner), lambda b, c: (0, 0))
    in_specs = [rows,
                pl.BlockSpec((ln, gn), lambda b, c: (rb(b, c), d_inner // gn)),
                pl.BlockSpec((ln, gn), lambda b, c: (rb(b, c), d_inner // gn + 1)),
                pl.BlockSpec((None, groups, ln, hg), lambda b, c: (d, 0, rb(b, c), 0)),
                pl.BlockSpec((None, groups, None, hg, ln), lambda b, c: (d, 0, rb(b, c), 0, 0))]
    args = [xbc_act, xbc_act, xbc_act, cum, src]
    if forward:
        y_bwd, z, d_skip_e, norm_w = epilogue_inputs
        in_specs += [rows, rows, vec, vec]
        args += [y_bwd, z, d_skip_e.reshape(1, d_inner), norm_w.reshape(1, d_inner)]
    est = (2 * (4 if forward else 2) * _nbytes((ln, d_inner), BF16) + 4 * _nbytes((ln, gn), BF16)
           + _nbytes((n, d_inner), F32) + 10 * _nbytes((ln, gw), F32) + 16 * _nbytes((ln, ln), F32))
    return pl.pallas_call(
        functools.partial(_ssd_body, forward=forward, groups=groups, hg=hg, p=p),
        grid=(bsz, nc),
        in_specs=in_specs,
        out_specs=rows,
        out_shape=jax.ShapeDtypeStruct((t, d_inner), BF16),
        scratch_shapes=[pltpu.VMEM((groups, n, gw), F32), pltpu.VMEM((ln, gw), F32)],
        compiler_params=_params(("parallel", "arbitrary"), est),
    )(*args)


def _rope_tables(seq, dh):
    half = dh // 2
    quarter = half // 2
    rows = seq // GRID_W
    row = jnp.repeat(jnp.arange(rows, dtype=jnp.int32), GRID_W).astype(F32)
    col = jnp.tile(jnp.arange(GRID_W, dtype=jnp.int32), rows).astype(F32)
    inv = 1.0 / (ROPE_THETA ** (jnp.arange(quarter, dtype=F32) / quarter))
    ang_r = row[:, None] * inv[None, :]
    ang_c = col[:, None] * inv[None, :]
    cos = jnp.concatenate([jnp.cos(ang_r), jnp.cos(ang_c), jnp.cos(ang_r), jnp.cos(ang_c)], axis=-1)
    sin = jnp.concatenate([-jnp.sin(ang_r), -jnp.sin(ang_c), jnp.sin(ang_r), jnp.sin(ang_c)], axis=-1)
    return cos, sin


def _pair_major(a, dh):
    lead = a.shape[:-1]
    nh = a.shape[-1] // dh
    a = a.reshape(lead + (nh, 2, 2, dh // 4))
    return jnp.swapaxes(a, -3, -2).reshape(lead + (nh * dh,))


def _norm_rope_epilogue(accs, extras, *, dh, scale):
    acc = accs[0]
    w, cos, sin = extras
    w = w * scale
    ones = jnp.ones((dh, dh), BF16)
    outs = []
    for hd in range(acc.shape[1] // dh):
        v = acc[:, hd * dh:(hd + 1) * dh]
        ss = jnp.dot((v * v).astype(BF16), ones, preferred_element_type=F32)
        v = v * lax.rsqrt(ss * (1.0 / dh) + RMS_EPS) * w
        outs.append(v * cos + pltpu.roll(v, dh // 2, 1) * sin)
    return [jnp.concatenate(outs, axis=1)]


def _proj_norm_rope(h, w, norm_w, cos, sin, seq, dh, scale, tm):
    n = w.shape[1]
    tn = _pick(n, (512, 256, 128))
    per_seq = seq // tm
    specs = [pl.BlockSpec((1, dh), lambda i, j: (0, 0)),
             pl.BlockSpec((tm, dh), lambda i, j: (i % per_seq, 0)),
             pl.BlockSpec((tm, dh), lambda i, j: (i % per_seq, 0))]
    (out,) = _fused_matmul([h], [_pair_major(w, dh)], [_pair_major(norm_w, dh).reshape(1, dh), cos, sin], specs,
                           functools.partial(_norm_rope_epilogue, dh=dh, scale=scale), [BF16], tm, tn)
    return out


def _attn_body(q_ref, k_ref, v_ref, o_ref, *, dh, kv_per_step):
    g_per_kv = q_ref.shape[1] // dh // kv_per_step
    for g in range(q_ref.shape[1] // dh):
        kv = g // g_per_kv
        k = k_ref[:, kv * dh:(kv + 1) * dh]
        v = v_ref[:, kv * dh:(kv + 1) * dh]
        sl = slice(g * dh, (g + 1) * dh)
        s = lax.dot_general(q_ref[:, sl], k, (((1,), (1,)), ((), ())), preferred_element_type=F32)
        e = jnp.exp2(s - jnp.max(s, axis=-1, keepdims=True))
        den = jnp.sum(e, axis=-1, keepdims=True)
        o = jnp.dot(e.astype(BF16), v, preferred_element_type=F32)
        o_ref[:, sl] = (o / den).astype(o_ref.dtype)


def _attention(q, k, v, bsz, seq, dh):
    t, q_cols = q.shape
    kv_heads = k.shape[1] // dh
    kv_per_step = 2 if kv_heads % 2 == 0 else 1
    gq = q_cols // kv_heads * kv_per_step
    tq = _pick(seq, (256, 128, 64, 32, 16))
    nq = seq // tq
    est = (4 * kv_per_step * _nbytes((seq, dh), BF16) + 4 * _nbytes((tq, gq), BF16)
           + 4 * _nbytes((tq, seq), F32) + 2 * _nbytes((tq, seq), BF16))
    return pl.pallas_call(
        functools.partial(_attn_body, dh=dh, kv_per_step=kv_per_step),
        grid=(bsz, kv_heads // kv_per_step, nq),
        in_specs=[pl.BlockSpec((tq, gq), lambda b, j, i: (b * nq + i, j)),
                  pl.BlockSpec((seq, kv_per_step * dh), lambda b, j, i: (b, j)),
                  pl.BlockSpec((seq, kv_per_step * dh), lambda b, j, i: (b, j))],
        out_specs=pl.BlockSpec((tq, gq), lambda b, j, i: (b * nq + i, j)),
        out_shape=jax.ShapeDtypeStruct((t, q_cols), BF16),
        compiler_params=_params(("parallel", "parallel", "arbitrary"), est),
    )(q, k, v)


def _plain(accs, extras):
    return [accs[0]]


def _gate_epilogue(accs, extras):
    return [_sigmoid(accs[0] + extras[0])]


def _mix_epilogue(accs, extras):
    return [extras[0].astype(F32) * accs[0] + extras[1].astype(F32) * accs[1]]


def kernel(x, c, w_ada, b_ada, w_in, ssm_conv_w, ssm_conv_b, ssm_a_log, ssm_dt_bias, ssm_d, ssm_norm_w,
           q_norm_w, k_norm_w, w_ssm_proj, w_attn_proj, w_gate, b_gate, w_out, ln1_g, ln1_b, w_up,
           ffn_conv_w, ffn_conv_b, w_down, ln2_g, ln2_b):
    bsz, seq, d = x.shape
    depth = w_ada.shape[0]
    t = bsz * seq
    d_inner = w_ssm_proj.shape[1]
    heads = ssm_a_log.shape[2]
    conv_dim = ssm_conv_w.shape[2]
    groups = (conv_dim - d_inner) // (2 * SSM_STATE)
    q_cols = w_attn_proj.shape[1]
    dh = q_norm_w.shape[1]
    kv_cols = (w_in.shape[2] - d_inner - conv_dim - 2 * heads - q_cols) // 2
    ffn = w_down.shape[1]
    alpha = (2.0 * depth) ** 0.25
    o_xbc = d_inner
    o_dt = o_xbc + conv_dim
    o_q = o_dt + 2 * heads
    o_k = o_q + q_cols
    o_v = o_k + kv_cols

    tm = _pick(seq, (1024, 512, 256, 128))
    cos, sin = _rope_tables(seq, dh)
    x2 = x.reshape(t, d)

    for layer in range(depth):
        wi = w_in[layer].astype(BF16)
        w_q = wi[:, o_q:o_k]
        w_k = wi[:, o_k:o_v]

        mod3 = _ada_mod(c, w_ada[layer], b_ada[layer]).reshape(bsz, 1, 6 * d)
        h = _ln_modulate(x2, mod3, seq, 0, 1)

        def proj(col0, width, dtype):
            tn = _pick(math.gcd(col0, width), (1024, 512, 256, 128))
            return _fused_matmul([h], [wi], [], [], _plain, [dtype], tm, tn, b_cols=(col0, width))[0]

        z = proj(0, o_xbc, BF16)
        dt_raw = proj(o_dt, 2 * heads, F32)
        v = proj(o_v, kv_cols, BF16)
        q = _proj_norm_rope(h, w_q, q_norm_w[layer], cos, sin, seq, dh, dh ** -0.5 * LOG2_E, tm)
        k = _proj_norm_rope(h, w_k, k_norm_w[layer], cos, sin, seq, dh, 1.0, tm)

        xbc_act = _proj_conv_act(h, wi, ssm_conv_w[layer], ssm_conv_b[layer], seq, glu=False, col0=o_xbc)
        cum, src = _ssd_steps(dt_raw, ssm_dt_bias[layer], ssm_a_log[layer], groups)
        y_bwd = _ssd_scan(xbc_act, cum, src, bsz, seq, d_inner, groups, forward=False)
        d_skip_e = jnp.repeat(ssm_d[layer], d_inner // heads)
        y_ssm = _ssd_scan(xbc_act, cum, src, bsz, seq, d_inner, groups, forward=True,
                          epilogue_inputs=(y_bwd, z, d_skip_e, ssm_norm_w[layer]))

        y_attn = _attention(q, k, v, bsz, seq, dh)

        tn_d = _pick(d, (512, 256, 128))
        tn_wide = _pick(d, (1024, 512, 256, 128))
        wg = w_gate[layer].astype(BF16)
        bg = b_gate[layer].reshape(1, 2 * d)
        (gates,) = _fused_matmul([h], [wg], [bg], [pl.BlockSpec((1, tn_wide), lambda i, j: (0, j))],
                                 _gate_epilogue, [BF16], tm, tn_wide)
        nb = d // tn_d
        w_sp = w_ssm_proj[layer].astype(BF16)
        w_ap = w_attn_proj[layer].astype(BF16)
        tm_mix = _fit_tm(tm, tn_d, [y_ssm, y_attn], [w_sp, w_ap], 2, [BF16])
        (mixed_pre,) = _fused_matmul(
            [y_ssm, y_attn], [w_sp, w_ap], [gates, gates],
            [pl.BlockSpec((tm_mix, tn_d), lambda i, j: (i, j)),
             pl.BlockSpec((tm_mix, tn_d), lambda i, j: (i, j + nb))],
            _mix_epilogue, [BF16], tm_mix, tn_d)
        mixed = _fused_matmul([mixed_pre], [w_out[layer].astype(BF16)], [], [], _plain, [BF16], tm, tn_wide)[0]
        x2, h2 = _residual_ln(x2, mixed, mod3, 2, ln1_g[layer], ln1_b[layer], seq, alpha, h_idx=(3, 4))

        w_up_b = w_up[layer].astype(BF16)
        act = _proj_conv_act(h2, w_up_b, ffn_conv_w[layer], ffn_conv_b[layer], seq, glu=True)
        w_dn = w_down[layer].astype(BF16)
        tm_dn = _fit_tm(tm, tn_d, [act], [w_dn], 0, [BF16])
        f = _fused_matmul([act], [w_dn], [], [], _plain, [BF16], tm_dn, tn_d)[0]
        x2 = _residual_ln(x2, f, mod3, 5, ln2_g[layer], ln2_b[layer], seq, alpha)

    return x2.reshape(bsz, seq, d)
```

```python
import functools
import math

import jax
import jax.numpy as jnp
from jax import lax
from jax.experimental import pallas as pl
from jax.experimental.pallas import tpu as pltpu

F32 = jnp.float32
BF16 = jnp.bfloat16

GRID_W = 64
SSM_STATE = 128
SSM_CHUNK = 128
ROPE_THETA = 10000.0
LN_EPS = 1e-5
RMS_EPS = 1e-6
LOG2_E = 1.4426950408889634

V7X_LANES = 128
V7X_MXU_DIM = 256
V7X_F32_SUBLANES = 8
V7X_BF16_SUBLANES = 16
V7X_VMEM_BYTES = 64 * 1024 * 1024
VMEM_CEILING = V7X_VMEM_BYTES - 6 * 1024 * 1024


def _pick(n, cands):
    for c in cands:
        if n % c == 0:
            return c
    return n


def _params(sem, est_bytes):
    limit = int(min(max(est_bytes, 32 * 1024 * 1024), VMEM_CEILING))
    return pltpu.CompilerParams(dimension_semantics=sem, vmem_limit_bytes=limit)


def _nbytes(shape, dtype):
    n = 1
    for s in shape:
        n *= s
    return n * jnp.dtype(dtype).itemsize


def _sigmoid(v):
    return 1.0 / (1.0 + jnp.exp(-v))


def _silu(v):
    return v * _sigmoid(v)


def _layer_norm(v):
    mu = jnp.mean(v, axis=-1, keepdims=True)
    vc = v - mu
    var = jnp.mean(vc * vc, axis=-1, keepdims=True)
    return vc * lax.rsqrt(var + LN_EPS)


def _mm_body(*refs, n_pairs, n_extras, prologue, epilogue):
    a_refs = refs[:n_pairs]
    b_refs = refs[n_pairs:2 * n_pairs]
    e_refs = refs[2 * n_pairs:2 * n_pairs + n_extras]
    o_refs = refs[2 * n_pairs + n_extras:]
    accs = []
    for a_ref, b_ref in zip(a_refs, b_refs):
        a = a_ref[...]
        if prologue is not None:
            a = prologue(a)
        accs.append(jnp.dot(a, b_ref[...].astype(BF16), preferred_element_type=F32))
    outs = epilogue(accs, [e[...] for e in e_refs])
    for o_ref, v in zip(o_refs, outs):
        o_ref[...] = v.astype(o_ref.dtype)


def _mm_vmem(tm, tn, a_list, b_list, n_extras, out_dtypes, has_prologue):
    est = sum(2 * _nbytes((tm, a.shape[1]), a.dtype) for a in a_list)
    est += sum(2 * _nbytes((b.shape[0], tn), b.dtype) for b in b_list)
    est += n_extras * 2 * _nbytes((tm, tn), F32)
    est += sum(2 * _nbytes((tm, tn), dt) for dt in out_dtypes)
    est += (len(a_list) + 2) * _nbytes((tm, tn), F32)
    if has_prologue or any(b.dtype != BF16 for b in b_list):
        est += max(_nbytes((b.shape[0], tn), BF16) for b in b_list) + _nbytes((tm, a_list[0].shape[1]), F32)
    return est


def _fit_tm(tm, tn, a_list, b_list, n_extras, out_dtypes):
    while tm > V7X_BF16_SUBLANES and _mm_vmem(tm, tn, a_list, b_list, n_extras, out_dtypes, False) > VMEM_CEILING:
        tm //= 2
    return tm


def _fused_matmul(a_list, b_list, extras, extra_specs, epilogue, out_dtypes, tm, tn, prologue=None,
                  b_cols=None):
    m = a_list[0].shape[0]
    col0, n = b_cols if b_cols is not None else (0, b_list[0].shape[1])
    assert m % tm == 0 and n % tn == 0 and col0 % tn == 0
    off = col0 // tn
    in_specs = ([pl.BlockSpec((tm, a.shape[1]), lambda i, j: (i, 0)) for a in a_list]
                + [pl.BlockSpec((b.shape[0], tn), lambda i, j: (0, j + off)) for b in b_list]
                + list(extra_specs))
    out_specs = [pl.BlockSpec((tm, tn), lambda i, j: (i, j)) for _ in out_dtypes]
    est = _mm_vmem(tm, tn, a_list, b_list, len(extras), out_dtypes, prologue is not None)
    body = functools.partial(_mm_body, n_pairs=len(a_list), n_extras=len(extras),
                             prologue=prologue, epilogue=epilogue)
    outs = pl.pallas_call(
        body,
        grid=(m // tm, n // tn),
        in_specs=in_specs,
        out_specs=out_specs,
        out_shape=[jax.ShapeDtypeStruct((m, n), dt) for dt in out_dtypes],
        compiler_params=_params(("parallel", "arbitrary"), est),
    )(*a_list, *b_list, *extras)
    return outs


def _ada_mod(c, w_ada, b_ada):
    bsz, d = c.shape
    rows = 8
    c_pad = jnp.zeros((rows, d), F32).at[:bsz].set(c)
    n = w_ada.shape[1]
    tn = _pick(n, (512, 256, 128))

    def prologue(a):
        return _silu(a).astype(BF16)

    def epilogue(accs, extras):
        return [accs[0] + extras[0]]

    (mod,) = _fused_matmul([c_pad], [w_ada], [b_ada.reshape(1, n)],
                           [pl.BlockSpec((1, tn), lambda i, j: (0, j))],
                           epilogue, [F32], rows, tn, prologue=prologue)
    return mod[:bsz]


def _ln_mod_body(x_ref, sh_ref, sc_ref, o_ref):
    y = _layer_norm(x_ref[...])
    o_ref[...] = (y * (1.0 + sc_ref[...]) + sh_ref[...]).astype(o_ref.dtype)


def _ln_modulate(x2, mod3, seq, shift_idx, scale_idx):
    t, d = x2.shape
    tm = _pick(seq, (256, 128, 64, 32, 16))
    per_seq = seq // tm
    est = 2 * _nbytes((tm, d), F32) + 2 * _nbytes((tm, d), BF16) + 4 * _nbytes((tm, d), F32)
    return pl.pallas_call(
        _ln_mod_body,
        grid=(t // tm,),
        in_specs=[pl.BlockSpec((tm, d), lambda i: (i, 0)),
                  pl.BlockSpec((None, 1, d), lambda i: (i // per_seq, 0, shift_idx)),
                  pl.BlockSpec((None, 1, d), lambda i: (i // per_seq, 0, scale_idx))],
        out_specs=pl.BlockSpec((tm, d), lambda i: (i, 0)),
        out_shape=jax.ShapeDtypeStruct((t, d), BF16),
        compiler_params=_params(("parallel",), est),
    )(x2, mod3, mod3)


def _res_ln_body(*refs, alpha, with_h):
    if with_h:
        x_ref, m_ref, gate_ref, g_ref, b_ref, sh_ref, sc_ref, o_ref, h_ref = refs
    else:
        x_ref, m_ref, gate_ref, g_ref, b_ref, o_ref = refs
    v = alpha * x_ref[...] + gate_ref[...] * m_ref[...].astype(F32)
    xn = _layer_norm(v) * g_ref[...] + b_ref[...]
    o_ref[...] = xn
    if with_h:
        h_ref[...] = (_layer_norm(xn) * (1.0 + sc_ref[...]) + sh_ref[...]).astype(h_ref.dtype)


def _residual_ln(x2, m2, mod3, gate_idx, ln_g, ln_b, seq, alpha, h_idx=None):
    t, d = x2.shape
    tm = _pick(seq, (256, 128, 64, 32, 16))
    per_seq = seq // tm
    with_h = h_idx is not None
    row = pl.BlockSpec((tm, d), lambda i: (i, 0))
    vec = pl.BlockSpec((1, d), lambda i: (0, 0))

    def modspec(k):
        return pl.BlockSpec((None, 1, d), lambda i: (i // per_seq, 0, k))

    in_specs = [row, row, modspec(gate_idx), vec, vec]
    args = [x2, m2, mod3, ln_g.reshape(1, d), ln_b.reshape(1, d)]
    out_specs = [row]
    out_shape = [jax.ShapeDtypeStruct((t, d), F32)]
    if with_h:
        in_specs += [modspec(h_idx[0]), modspec(h_idx[1])]
        args += [mod3, mod3]
        out_specs.append(row)
        out_shape.append(jax.ShapeDtypeStruct((t, d), BF16))
    est = 12 * _nbytes((tm, d), F32)
    outs = pl.pallas_call(
        functools.partial(_res_ln_body, alpha=alpha, with_h=with_h),
        grid=(t // tm,),
        in_specs=in_specs,
        out_specs=out_specs,
        out_shape=out_shape,
        compiler_params=_params(("parallel",), est),
    )(*args)
    return outs if with_h else outs[0]


def _shift_matrix(blk):
    i = jnp.arange(blk)[:, None]
    j = jnp.arange(blk)[None, :]
    return jnp.concatenate([j == i - 1, j == i + 1], axis=1).astype(BF16)


def _conv3(cur_ref, prev_ref, next_ref, w_ref, b_ref, shift, first, last, halo, rows):
    tm = cur_ref.shape[0]
    lo, hi = rows.start, rows.stop
    blk = hi - lo
    sub = V7X_F32_SUBLANES
    w = w_ref[...]
    w_side = w.astype(BF16)
    xb = cur_ref[lo:hi, :]
    side = jnp.concatenate([xb * w_side[0:1, :], xb * w_side[2:3, :]], axis=0)
    out = (jnp.dot(shift, side, preferred_element_type=F32)
           + (xb.astype(F32) * w[1:2, :] + b_ref[...]))
    if lo == 0:
        before = prev_ref[halo - 1:halo, :].astype(F32) * jnp.where(first, 0.0, 1.0)
    else:
        before = cur_ref[lo - 1:lo, :].astype(F32)
    if hi == tm:
        after = next_ref[0:1, :].astype(F32) * jnp.where(last, 0.0, 1.0)
    else:
        after = cur_ref[hi:hi + 1, :].astype(F32)
    row = lax.broadcasted_iota(jnp.int32, (sub, xb.shape[1]), 0)
    head = out[0:sub] + jnp.where(row == 0, before * w[0:1, :], 0.0)
    tail = out[blk - sub:blk] + jnp.where(row == sub - 1, after * w[2:3, :], 0.0)
    return jnp.concatenate([head, out[sub:blk - sub], tail], axis=0)


def _conv_body(*refs, per_seq, glu, halo, blk):
    i = pl.program_id(1)
    first = (i % per_seq) == 0
    last = (i % per_seq) == per_seq - 1
    shift = refs[0][...]
    o_ref = refs[-1]
    tm = o_ref.shape[0]
    for lo in range(0, tm, blk):
        rows = slice(lo, lo + blk)
        if glu:
            ca, pa, na, wa, ba, cb, pb, nb, wb, bb = refs[1:-1]
            ua = _conv3(ca, pa, na, wa, ba, shift, first, last, halo, rows)
            ub = _conv3(cb, pb, nb, wb, bb, shift, first, last, halo, rows)
            o_ref[rows, :] = (_silu(ua) * ub).astype(o_ref.dtype)
        else:
            ca, pa, na, wa, ba = refs[1:-1]
            o_ref[rows, :] = _silu(_conv3(ca, pa, na, wa, ba, shift, first, last, halo, rows)).astype(o_ref.dtype)


def _dwconv_act(u, w, b, seq, glu):
    t, c_in = u.shape
    c_out = c_in // 2 if glu else c_in
    halo = V7X_BF16_SUBLANES
    tm = _pick(seq, (2048, 1024, 512, 256, 128, 64, 32, 16))
    tc = _pick(c_out, (512, 256, 128))
    blk = min(tm, V7X_MXU_DIM)
    per_seq = seq // tm
    hb = tm // halo
    n_halo = t // halo
    b2 = b.reshape(1, c_in)

    def group(off):
        return [pl.BlockSpec((tm, tc), lambda j, i: (i, j + off)),
                pl.BlockSpec((halo, tc), lambda j, i: (jnp.maximum(i * hb - 1, 0), j + off)),
                pl.BlockSpec((halo, tc), lambda j, i: (jnp.minimum((i + 1) * hb, n_halo - 1), j + off)),
                pl.BlockSpec((3, tc), lambda j, i: (0, j + off)),
                pl.BlockSpec((1, tc), lambda j, i: (0, j + off))]

    in_specs = [pl.BlockSpec((blk, 2 * blk), lambda j, i: (0, 0))] + group(0)
    args = [_shift_matrix(blk), u, u, u, w, b2]
    if glu:
        in_specs += group(c_out // tc)
        args += [u, u, u, w, b2]
    est = ((2 if glu else 1) * (2 * _nbytes((tm, tc), BF16) + 8 * _nbytes((blk, tc), F32))
           + 2 * _nbytes((tm, tc), BF16))
    return pl.pallas_call(
        functools.partial(_conv_body, per_seq=per_seq, glu=glu, halo=halo, blk=blk),
        grid=(c_out // tc, t // tm),
        in_specs=in_specs,
        out_specs=pl.BlockSpec((tm, tc), lambda j, i: (i, j)),
        out_shape=jax.ShapeDtypeStruct((t, c_out), BF16),
        compiler_params=_params(("parallel", "parallel"), est),
    )(*args)


def _split3_dot(tri, a):
    a1 = a.astype(BF16)
    r1 = a - a1.astype(F32)
    a2 = r1.astype(BF16)
    a3 = (r1 - a2.astype(F32)).astype(BF16)
    dot = functools.partial(jnp.dot, preferred_element_type=F32)
    return dot(tri, a1) + dot(tri, a2) + dot(tri, a3)


def _dt_body(raw_ref, bias_ref, alog_ref, cum_ref, src_ref, *, heads, groups):
    raw = raw_ref[...]
    v = raw + bias_ref[...]
    dt = jnp.maximum(v, 0.0) + jnp.log1p(jnp.exp(-jnp.abs(v)))
    a = dt * (-jnp.exp(alog_ref[...]))
    ln = raw.shape[0]
    r = lax.broadcasted_iota(jnp.int32, (ln, ln), 0)
    s = lax.broadcasted_iota(jnp.int32, (ln, ln), 1)
    lower = jnp.where(r >= s, 1.0, 0.0).astype(BF16)
    upper = jnp.where(r <= s, 1.0, 0.0).astype(BF16)
    cum = jnp.concatenate([_split3_dot(lower, a[:, :heads]), _split3_dot(upper, a[:, heads:])], axis=1)
    cum2 = cum * LOG2_E
    src_t = (cum2 - jnp.log(dt) * LOG2_E).T
    hg = heads // groups
    for d in range(2):
        for g in range(groups):
            lo = d * heads + g * hg
            cum_ref[d, g] = cum2[:, lo:lo + hg]
            src_ref[d, g, 0] = src_t[lo:lo + hg, :]


def _ssd_steps(dt_raw, dt_bias, a_log, groups):
    t, h2 = dt_raw.shape
    heads = h2 // 2
    hg = heads // groups
    ln = SSM_CHUNK
    nct = t // ln
    return pl.pallas_call(
        functools.partial(_dt_body, heads=heads, groups=groups),
        grid=(nct,),
        in_specs=[pl.BlockSpec((ln, h2), lambda c: (c, 0)),
                  pl.BlockSpec((1, h2), lambda c: (0, 0)),
                  pl.BlockSpec((1, h2), lambda c: (0, 0))],
        out_specs=[pl.BlockSpec((2, groups, ln, hg), lambda c: (0, 0, c, 0)),
                   pl.BlockSpec((2, groups, 1, hg, ln), lambda c: (0, 0, c, 0, 0))],
        out_shape=[jax.ShapeDtypeStruct((2, groups, t, hg), F32),
                   jax.ShapeDtypeStruct((2, groups, nct, hg, ln), F32)],
        compiler_params=_params(("parallel",), 0),
    )(dt_raw, dt_bias.reshape(1, h2), a_log.reshape(1, h2))


def _ssd_body(*refs, forward, groups, hg, p):
    if forward:
        x_ref, b_ref, c_ref, cum_ref, src_ref, yb_ref, z_ref, dsk_ref, w_ref, o_ref, h_ref, ch_ref = refs
    else:
        x_ref, b_ref, c_ref, cum_ref, src_ref, o_ref, h_ref, ch_ref = refs

    @pl.when(pl.program_id(1) == 0)
    def _():
        h_ref[...] = jnp.zeros_like(h_ref)

    ln = x_ref.shape[0]
    n = SSM_STATE
    width = 2 * p
    gw = hg * p
    r = lax.broadcasted_iota(jnp.int32, (ln, ln), 0)
    s = lax.broadcasted_iota(jnp.int32, (ln, ln), 1)
    mask = (r >= s) if forward else (r <= s)
    end_row = ln - 1 if forward else 0
    low = lax.broadcasted_iota(jnp.int32, (ln, width), 1) < p
    zero = jnp.zeros((ln, width), x_ref.dtype)

    def group(g, carry):
        gs = pl.ds(pl.multiple_of(g * gw, gw), gw)
        ns = pl.ds(pl.multiple_of(g * n, n), n)
        bm = b_ref[:, ns]
        cm = c_ref[:, ns]
        cb = lax.dot_general(cm, bm, (((1,), (1,)), ((), ())), preferred_element_type=F32)
        ch_ref[...] = jnp.dot(cm, h_ref[g].astype(BF16), preferred_element_type=F32)
        bm_t = bm.astype(F32).T
        cb16 = cb.astype(BF16)

        for i in range(hg // 2):
            ja, jb = 2 * i, 2 * i + 1
            sl = slice(i * width, (i + 1) * width)
            xs = pl.ds(pl.multiple_of(g * gw + i * width, width), width)
            cum_a = jnp.broadcast_to(cum_ref[g, :, ja:ja + 1], (ln, ln))
            cum_b = jnp.broadcast_to(cum_ref[g, :, jb:jb + 1], (ln, ln))
            src_a = src_ref[g, ja:ja + 1, :]
            src_b = src_ref[g, jb:jb + 1, :]
            m_a = cb16 * jnp.exp2(jnp.where(mask, cum_a - src_a, -jnp.inf)).astype(BF16)
            m_b = cb16 * jnp.exp2(jnp.where(mask, cum_b - src_b, -jnp.inf)).astype(BF16)
            xp = x_ref[:, xs]
            x_ab = jnp.concatenate([jnp.where(low, xp, zero), jnp.where(low, zero, xp)], axis=0)
            cum_e = jnp.where(low, cum_a, cum_b)
            y = (jnp.dot(jnp.concatenate([m_a, m_b], axis=1), x_ab, preferred_element_type=F32)
                 + jnp.exp2(cum_e) * ch_ref[:, sl])
            if forward:
                ch_ref[:, sl] = y
            else:
                o_ref[:, xs] = y.astype(o_ref.dtype)
            end_a = cum_ref[g, end_row:end_row + 1, ja:ja + 1]
            end_b = cum_ref[g, end_row:end_row + 1, jb:jb + 1]
            bt_ab = jnp.concatenate([(bm_t * jnp.exp2(end_a - src_a)).astype(BF16),
                                     (bm_t * jnp.exp2(end_b - src_b)).astype(BF16)], axis=1)
            h_ref[g, :, sl] = (h_ref[g, :, sl] * jnp.exp2(cum_e[end_row:end_row + 1, :])
                               + jnp.dot(bt_ab, x_ab, preferred_element_type=F32))

        if forward:
            y = ch_ref[...] + yb_ref[:, gs].astype(F32) + dsk_ref[:, gs] * x_ref[:, gs].astype(F32)
            yg = y * _silu(z_ref[:, gs].astype(F32))
            yg = yg * lax.rsqrt(jnp.mean(yg * yg, axis=-1, keepdims=True) + RMS_EPS)
            o_ref[:, gs] = (yg * w_ref[:, gs]).astype(o_ref.dtype)
        return carry

    lax.fori_loop(0, groups, group, 0, unroll=8)


def _ssd_scan(xbc_act, cum, src, bsz, seq, d_inner, groups, forward, epilogue_inputs=()):
    t = xbc_act.shape[0]
    n = SSM_STATE
    ln = SSM_CHUNK
    nc = seq // ln
    hg = cum.shape[3]
    heads = hg * groups
    p = d_inner // heads
    gw = hg * p
    gn = groups * n
    assert 2 * p == V7X_LANES and ln == V7X_LANES and hg % 2 == 0 and d_inner % gn == 0
    d = 0 if forward else 1

    def rb(b, c):
        return b * nc + (c if forward else nc - 1 - c)

    rows = pl.BlockSpec((ln, d_inner), lambda b, c: (rb(b, c), 0))
    vec = pl.BlockSpec((1, d_inner), lambda b, c: (0, 0))
    in_specs = [rows,
                pl.BlockSpec((ln, gn), lambda b, c: (rb(b, c), d_inner // gn)),
                pl.BlockSpec((ln, gn), lambda b, c: (rb(b, c), d_inner // gn + 1)),
                pl.BlockSpec((None, groups, ln, hg), lambda b, c: (d, 0, rb(b, c), 0)),
                pl.BlockSpec((None, groups, None, hg, ln), lambda b, c: (d, 0, rb(b, c), 0, 0))]
    args = [xbc_act, xbc_act, xbc_act, cum, src]
    if forward:
        y_bwd, z, d_skip_e, norm_w = epilogue_inputs
        in_specs += [rows, rows, vec, vec]
        args += [y_bwd, z, d_skip_e.reshape(1, d_inner), norm_w.reshape(1, d_inner)]
    est = (2 * (4 if forward else 2) * _nbytes((ln, d_inner), BF16) + 4 * _nbytes((ln, gn), BF16)
           + _nbytes((n, d_inner), F32) + 10 * _nbytes((ln, gw), F32) + 16 * _nbytes((ln, ln), F32))
    return pl.pallas_call(
        functools.partial(_ssd_body, forward=forward, groups=groups, hg=hg, p=p),
        grid=(bsz, nc),
        in_specs=in_specs,
        out_specs=rows,
        out_shape=jax.ShapeDtypeStruct((t, d_inner), BF16),
        scratch_shapes=[pltpu.VMEM((groups, n, gw), F32), pltpu.VMEM((ln, gw), F32)],
        compiler_params=_params(("parallel", "arbitrary"), est),
    )(*args)


def _rope_tables(seq, dh):
    half = dh // 2
    quarter = half // 2
    rows = seq // GRID_W
    row = jnp.repeat(jnp.arange(rows, dtype=jnp.int32), GRID_W).astype(F32)
    col = jnp.tile(jnp.arange(GRID_W, dtype=jnp.int32), rows).astype(F32)
    inv = 1.0 / (ROPE_THETA ** (jnp.arange(quarter, dtype=F32) / quarter))
    ang_r = row[:, None] * inv[None, :]
    ang_c = col[:, None] * inv[None, :]
    cos = jnp.concatenate([jnp.cos(ang_r), jnp.cos(ang_c), jnp.cos(ang_r), jnp.cos(ang_c)], axis=-1)
    sin = jnp.concatenate([-jnp.sin(ang_r), -jnp.sin(ang_c), jnp.sin(ang_r), jnp.sin(ang_c)], axis=-1)
    return cos, sin


def _pair_major(a, dh):
    lead = a.shape[:-1]
    nh = a.shape[-1] // dh
    a = a.reshape(lead + (nh, 2, 2, dh // 4))
    return jnp.swapaxes(a, -3, -2).reshape(lead + (nh * dh,))


def _norm_rope_epilogue(accs, extras, *, dh, scale):
    acc = accs[0]
    w, cos, sin = extras
    w = w * scale
    ones = jnp.ones((dh, dh), BF16)
    outs = []
    for hd in range(acc.shape[1] // dh):
        v = acc[:, hd * dh:(hd + 1) * dh]
        ss = jnp.dot((v * v).astype(BF16), ones, preferred_element_type=F32)
        v = v * lax.rsqrt(ss * (1.0 / dh) + RMS_EPS) * w
        outs.append(v * cos + pltpu.roll(v, dh // 2, 1) * sin)
    return [jnp.concatenate(outs, axis=1)]


def _proj_norm_rope(h, w, norm_w, cos, sin, seq, dh, scale, tm):
    n = w.shape[1]
    tn = _pick(n, (512, 256, 128))
    per_seq = seq // tm
    specs = [pl.BlockSpec((1, dh), lambda i, j: (0, 0)),
             pl.BlockSpec((tm, dh), lambda i, j: (i % per_seq, 0)),
             pl.BlockSpec((tm, dh), lambda i, j: (i % per_seq, 0))]
    (out,) = _fused_matmul([h], [_pair_major(w, dh)], [_pair_major(norm_w, dh).reshape(1, dh), cos, sin], specs,
                           functools.partial(_norm_rope_epilogue, dh=dh, scale=scale), [BF16], tm, tn)
    return out


def _attn_body(q_ref, k_ref, v_ref, o_ref, *, dh, kv_per_step):
    g_per_kv = q_ref.shape[1] // dh // kv_per_step
    for g in range(q_ref.shape[1] // dh):
        kv = g // g_per_kv
        k = k_ref[:, kv * dh:(kv + 1) * dh]
        v = v_ref[:, kv * dh:(kv + 1) * dh]
        sl = slice(g * dh, (g + 1) * dh)
        s = lax.dot_general(q_ref[:, sl], k, (((1,), (1,)), ((), ())), preferred_element_type=F32)
        e = jnp.exp2(s - jnp.max(s, axis=-1, keepdims=True))
        den = jnp.sum(e, axis=-1, keepdims=True)
        o = jnp.dot(e.astype(BF16), v, preferred_element_type=F32)
        o_ref[:, sl] = (o / den).astype(o_ref.dtype)


def _attention(q, k, v, bsz, seq, dh):
    t, q_cols = q.shape
    kv_heads = k.shape[1] // dh
    kv_per_step = 2 if kv_heads % 2 == 0 else 1
    gq = q_cols // kv_heads * kv_per_step
    tq = _pick(seq, (256, 128, 64, 32, 16))
    nq = seq // tq
    est = (4 * kv_per_step * _nbytes((seq, dh), BF16) + 4 * _nbytes((tq, gq), BF16)
           + 4 * _nbytes((tq, seq), F32) + 2 * _nbytes((tq, seq), BF16))
    return pl.pallas_call(
        functools.partial(_attn_body, dh=dh, kv_per_step=kv_per_step),
        grid=(bsz, kv_heads // kv_per_step, nq),
        in_specs=[pl.BlockSpec((tq, gq), lambda b, j, i: (b * nq + i, j)),
                  pl.BlockSpec((seq, kv_per_step * dh), lambda b, j, i: (b, j)),
                  pl.BlockSpec((seq, kv_per_step * dh), lambda b, j, i: (b, j))],
        out_specs=pl.BlockSpec((tq, gq), lambda b, j, i: (b * nq + i, j)),
        out_shape=jax.ShapeDtypeStruct((t, q_cols), BF16),
        compiler_params=_params(("parallel", "parallel", "arbitrary"), est),
    )(q, k, v)


def _plain(accs, extras):
    return [accs[0]]


def _gate_epilogue(accs, extras):
    return [_sigmoid(accs[0] + extras[0])]


def _mix_epilogue(accs, extras):
    return [extras[0].astype(F32) * accs[0] + extras[1].astype(F32) * accs[1]]


def kernel(x, c, w_ada, b_ada, w_in, ssm_conv_w, ssm_conv_b, ssm_a_log, ssm_dt_bias, ssm_d, ssm_norm_w,
           q_norm_w, k_norm_w, w_ssm_proj, w_attn_proj, w_gate, b_gate, w_out, ln1_g, ln1_b, w_up,
           ffn_conv_w, ffn_conv_b, w_down, ln2_g, ln2_b):
    bsz, seq, d = x.shape
    depth = w_ada.shape[0]
    t = bsz * seq
    d_inner = w_ssm_proj.shape[1]
    heads = ssm_a_log.shape[2]
    conv_dim = ssm_conv_w.shape[2]
    groups = (conv_dim - d_inner) // (2 * SSM_STATE)
    q_cols = w_attn_proj.shape[1]
    dh = q_norm_w.shape[1]
    kv_cols = (w_in.shape[2] - d_inner - conv_dim - 2 * heads - q_cols) // 2
    ffn = w_down.shape[1]
    alpha = (2.0 * depth) ** 0.25
    o_xbc = d_inner
    o_dt = o_xbc + conv_dim
    o_q = o_dt + 2 * heads
    o_k = o_q + q_cols
    o_v = o_k + kv_cols

    tm = _pick(seq, (1024, 512, 256, 128))
    cos, sin = _rope_tables(seq, dh)
    x2 = x.reshape(t, d)

    for layer in range(depth):
        wi = w_in[layer].astype(BF16)
        w_q = wi[:, o_q:o_k]
        w_k = wi[:, o_k:o_v]

        mod3 = _ada_mod(c, w_ada[layer], b_ada[layer]).reshape(bsz, 1, 6 * d)
        h = _ln_modulate(x2, mod3, seq, 0, 1)

        def proj(col0, width, dtype):
            tn = _pick(math.gcd(col0, width), (1024, 512, 256, 128))
            return _fused_matmul([h], [wi], [], [], _plain, [dtype], tm, tn, b_cols=(col0, width))[0]

        z = proj(0, o_xbc, BF16)
        xbc = proj(o_xbc, conv_dim, BF16)
        dt_raw = proj(o_dt, 2 * heads, F32)
        v = _fused_matmul([h], [wi[:, o_v:]], [], [], _plain, [BF16], tm,
                          _pick(kv_cols, (1024, 512, 256, 128)))[0]
        q = _proj_norm_rope(h, w_q, q_norm_w[layer], cos, sin, seq, dh, dh ** -0.5 * LOG2_E, tm)
        k = _proj_norm_rope(h, w_k, k_norm_w[layer], cos, sin, seq, dh, 1.0, tm)

        xbc_act = _dwconv_act(xbc, ssm_conv_w[layer], ssm_conv_b[layer], seq, glu=False)
        cum, src = _ssd_steps(dt_raw, ssm_dt_bias[layer], ssm_a_log[layer], groups)
        y_bwd = _ssd_scan(xbc_act, cum, src, bsz, seq, d_inner, groups, forward=False)
        d_skip_e = jnp.repeat(ssm_d[layer], d_inner // heads)
        y_ssm = _ssd_scan(xbc_act, cum, src, bsz, seq, d_inner, groups, forward=True,
                          epilogue_inputs=(y_bwd, z, d_skip_e, ssm_norm_w[layer]))

        y_attn = _attention(q, k, v, bsz, seq, dh)

        tn_d = _pick(d, (512, 256, 128))
        tn_wide = _pick(d, (1024, 512, 256, 128))
        wg = w_gate[layer].astype(BF16)
        bg = b_gate[layer].reshape(1, 2 * d)
        (gates,) = _fused_matmul([h], [wg], [bg], [pl.BlockSpec((1, tn_wide), lambda i, j: (0, j))],
                                 _gate_epilogue, [BF16], tm, tn_wide)
        nb = d // tn_d
        w_sp = w_ssm_proj[layer].astype(BF16)
        w_ap = w_attn_proj[layer].astype(BF16)
        tm_mix = _fit_tm(tm, tn_d, [y_ssm, y_attn], [w_sp, w_ap], 2, [BF16])
        (mixed_pre,) = _fused_matmul(
            [y_ssm, y_attn], [w_sp, w_ap], [gates, gates],
            [pl.BlockSpec((tm_mix, tn_d), lambda i, j: (i, j)),
             pl.BlockSpec((tm_mix, tn_d), lambda i, j: (i, j + nb))],
            _mix_epilogue, [BF16], tm_mix, tn_d)
        mixed = _fused_matmul([mixed_pre], [w_out[layer].astype(BF16)], [], [], _plain, [BF16], tm, tn_wide)[0]
        x2, h2 = _residual_ln(x2, mixed, mod3, 2, ln1_g[layer], ln1_b[layer], seq, alpha, h_idx=(3, 4))

        w_up_b = w_up[layer].astype(BF16)
        tn_up = _pick(2 * ffn, (1024, 512, 256, 128))
        tm_up = _fit_tm(_pick(seq, (2048, 1024, 512, 256, 128)), tn_up, [h2], [w_up_b], 0, [BF16])
        u = _fused_matmul([h2], [w_up_b], [], [], _plain, [BF16], tm_up, tn_up)[0]
        act = _dwconv_act(u, ffn_conv_w[layer], ffn_conv_b[layer], seq, glu=True)
        w_dn = w_down[layer].astype(BF16)
        tm_dn = _fit_tm(tm, tn_d, [act], [w_dn], 0, [BF16])
        f = _fused_matmul([act], [w_dn], [], [], _plain, [BF16], tm_dn, tn_d)[0]
        x2 = _residual_ln(x2, f, mod3, 5, ln2_g[layer], ln2_b[layer], seq, alpha)

    return x2.reshape(bsz, seq, d)
```

```python
import functools
import math

import jax
import jax.numpy as jnp
from jax import lax
from jax.experimental import pallas as pl
from jax.experimental.pallas import tpu as pltpu

F32 = jnp.float32
BF16 = jnp.bfloat16

GRID_W = 64
SSM_STATE = 128
SSM_CHUNK = 128
ROPE_THETA = 10000.0
LN_EPS = 1e-5
RMS_EPS = 1e-6
LOG2_E = 1.4426950408889634

V7X_LANES = 128
V7X_MXU_DIM = 256
V7X_F32_SUBLANES = 8
V7X_BF16_SUBLANES = 16
V7X_VMEM_BYTES = 64 * 1024 * 1024
VMEM_CEILING = V7X_VMEM_BYTES - 6 * 1024 * 1024


def _pick(n, cands):
    for c in cands:
        if n % c == 0:
            return c
    return n


def _params(sem, est_bytes):
    limit = int(min(max(est_bytes, 32 * 1024 * 1024), VMEM_CEILING))
    return pltpu.CompilerParams(dimension_semantics=sem, vmem_limit_bytes=limit)


def _nbytes(shape, dtype):
    n = 1
    for s in shape:
        n *= s
    return n * jnp.dtype(dtype).itemsize


def _sigmoid(v):
    return 1.0 / (1.0 + jnp.exp(-v))


def _silu(v):
    return v * _sigmoid(v)


def _layer_norm(v):
    mu = jnp.mean(v, axis=-1, keepdims=True)
    vc = v - mu
    var = jnp.mean(vc * vc, axis=-1, keepdims=True)
    return vc * lax.rsqrt(var + LN_EPS)


def _mm_body(*refs, n_pairs, n_extras, prologue, epilogue):
    a_refs = refs[:n_pairs]
    b_refs = refs[n_pairs:2 * n_pairs]
    e_refs = refs[2 * n_pairs:2 * n_pairs + n_extras]
    o_refs = refs[2 * n_pairs + n_extras:]
    accs = []
    for a_ref, b_ref in zip(a_refs, b_refs):
        a = a_ref[...]
        if prologue is not None:
            a = prologue(a)
        accs.append(jnp.dot(a, b_ref[...].astype(BF16), preferred_element_type=F32))
    outs = epilogue(accs, [e[...] for e in e_refs])
    for o_ref, v in zip(o_refs, outs):
        o_ref[...] = v.astype(o_ref.dtype)


def _mm_vmem(tm, tn, a_list, b_list, n_extras, out_dtypes, has_prologue):
    est = sum(2 * _nbytes((tm, a.shape[1]), a.dtype) for a in a_list)
    est += sum(2 * _nbytes((b.shape[0], tn), b.dtype) for b in b_list)
    est += n_extras * 2 * _nbytes((tm, tn), F32)
    est += sum(2 * _nbytes((tm, tn), dt) for dt in out_dtypes)
    est += (len(a_list) + 2) * _nbytes((tm, tn), F32)
    if has_prologue or any(b.dtype != BF16 for b in b_list):
        est += max(_nbytes((b.shape[0], tn), BF16) for b in b_list) + _nbytes((tm, a_list[0].shape[1]), F32)
    return est


def _fit_tm(tm, tn, a_list, b_list, n_extras, out_dtypes):
    while tm > V7X_BF16_SUBLANES and _mm_vmem(tm, tn, a_list, b_list, n_extras, out_dtypes, False) > VMEM_CEILING:
        tm //= 2
    return tm


def _fused_matmul(a_list, b_list, extras, extra_specs, epilogue, out_dtypes, tm, tn, prologue=None,
                  b_cols=None):
    m = a_list[0].shape[0]
    col0, n = b_cols if b_cols is not None else (0, b_list[0].shape[1])
    assert m % tm == 0 and n % tn == 0 and col0 % tn == 0
    off = col0 // tn
    in_specs = ([pl.BlockSpec((tm, a.shape[1]), lambda i, j: (i, 0)) for a in a_list]
                + [pl.BlockSpec((b.shape[0], tn), lambda i, j: (0, j + off)) for b in b_list]
                + list(extra_specs))
    out_specs = [pl.BlockSpec((tm, tn), lambda i, j: (i, j)) for _ in out_dtypes]
    est = _mm_vmem(tm, tn, a_list, b_list, len(extras), out_dtypes, prologue is not None)
    body = functools.partial(_mm_body, n_pairs=len(a_list), n_extras=len(extras),
                             prologue=prologue, epilogue=epilogue)
    outs = pl.pallas_call(
        body,
        grid=(m // tm, n // tn),
        in_specs=in_specs,
        out_specs=out_specs,
        out_shape=[jax.ShapeDtypeStruct((m, n), dt) for dt in out_dtypes],
        compiler_params=_params(("parallel", "arbitrary"), est),
    )(*a_list, *b_list, *extras)
    return outs


def _ada_mod(c, w_ada, b_ada):
    bsz, d = c.shape
    rows = 8
    c_pad = jnp.zeros((rows, d), F32).at[:bsz].set(c)
    n = w_ada.shape[1]
    tn = _pick(n, (512, 256, 128))

    def prologue(a):
        return _silu(a).astype(BF16)

    def epilogue(accs, extras):
        return [accs[0] + extras[0]]

    (mod,) = _fused_matmul([c_pad], [w_ada], [b_ada.reshape(1, n)],
                           [pl.BlockSpec((1, tn), lambda i, j: (0, j))],
                           epilogue, [F32], rows, tn, prologue=prologue)
    return mod[:bsz]


def _ln_mod_body(x_ref, sh_ref, sc_ref, o_ref):
    y = _layer_norm(x_ref[...])
    o_ref[...] = (y * (1.0 + sc_ref[...]) + sh_ref[...]).astype(o_ref.dtype)


def _ln_modulate(x2, mod3, seq, shift_idx, scale_idx):
    t, d = x2.shape
    tm = _pick(seq, (256, 128, 64, 32, 16))
    per_seq = seq // tm
    est = 2 * _nbytes((tm, d), F32) + 2 * _nbytes((tm, d), BF16) + 4 * _nbytes((tm, d), F32)
    return pl.pallas_call(
        _ln_mod_body,
        grid=(t // tm,),
        in_specs=[pl.BlockSpec((tm, d), lambda i: (i, 0)),
                  pl.BlockSpec((None, 1, d), lambda i: (i // per_seq, 0, shift_idx)),
                  pl.BlockSpec((None, 1, d), lambda i: (i // per_seq, 0, scale_idx))],
        out_specs=pl.BlockSpec((tm, d), lambda i: (i, 0)),
        out_shape=jax.ShapeDtypeStruct((t, d), BF16),
        compiler_params=_params(("parallel",), est),
    )(x2, mod3, mod3)


def _res_ln_body(*refs, alpha, with_h):
    if with_h:
        x_ref, m_ref, gate_ref, g_ref, b_ref, sh_ref, sc_ref, o_ref, h_ref = refs
    else:
        x_ref, m_ref, gate_ref, g_ref, b_ref, o_ref = refs
    v = alpha * x_ref[...] + gate_ref[...] * m_ref[...].astype(F32)
    xn = _layer_norm(v) * g_ref[...] + b_ref[...]
    o_ref[...] = xn
    if with_h:
        h_ref[...] = (_layer_norm(xn) * (1.0 + sc_ref[...]) + sh_ref[...]).astype(h_ref.dtype)


def _residual_ln(x2, m2, mod3, gate_idx, ln_g, ln_b, seq, alpha, h_idx=None):
    t, d = x2.shape
    tm = _pick(seq, (256, 128, 64, 32, 16))
    per_seq = seq // tm
    with_h = h_idx is not None
    row = pl.BlockSpec((tm, d), lambda i: (i, 0))
    vec = pl.BlockSpec((1, d), lambda i: (0, 0))

    def modspec(k):
        return pl.BlockSpec((None, 1, d), lambda i: (i // per_seq, 0, k))

    in_specs = [row, row, modspec(gate_idx), vec, vec]
    args = [x2, m2, mod3, ln_g.reshape(1, d), ln_b.reshape(1, d)]
    out_specs = [row]
    out_shape = [jax.ShapeDtypeStruct((t, d), F32)]
    if with_h:
        in_specs += [modspec(h_idx[0]), modspec(h_idx[1])]
        args += [mod3, mod3]
        out_specs.append(row)
        out_shape.append(jax.ShapeDtypeStruct((t, d), BF16))
    est = 12 * _nbytes((tm, d), F32)
    outs = pl.pallas_call(
        functools.partial(_res_ln_body, alpha=alpha, with_h=with_h),
        grid=(t // tm,),
        in_specs=in_specs,
        out_specs=out_specs,
        out_shape=out_shape,
        compiler_params=_params(("parallel",), est),
    )(*args)
    return outs if with_h else outs[0]


def _shift_matrix(blk):
    i = jnp.arange(blk)[:, None]
    j = jnp.arange(blk)[None, :]
    return jnp.concatenate([j == i - 1, j == i + 1], axis=1).astype(BF16)


def _conv3(cur_ref, prev_ref, next_ref, w_ref, b_ref, shift, first, last, halo, rows):
    tm = cur_ref.shape[0]
    lo, hi = rows.start, rows.stop
    blk = hi - lo
    sub = V7X_F32_SUBLANES
    w = w_ref[...]
    w_side = w.astype(BF16)
    xb = cur_ref[lo:hi, :]
    side = jnp.concatenate([xb * w_side[0:1, :], xb * w_side[2:3, :]], axis=0)
    out = (jnp.dot(shift, side, preferred_element_type=F32)
           + (xb.astype(F32) * w[1:2, :] + b_ref[...]))
    if lo == 0:
        before = prev_ref[halo - 1:halo, :].astype(F32) * jnp.where(first, 0.0, 1.0)
    else:
        before = cur_ref[lo - 1:lo, :].astype(F32)
    if hi == tm:
        after = next_ref[0:1, :].astype(F32) * jnp.where(last, 0.0, 1.0)
    else:
        after = cur_ref[hi:hi + 1, :].astype(F32)
    row = lax.broadcasted_iota(jnp.int32, (sub, xb.shape[1]), 0)
    head = out[0:sub] + jnp.where(row == 0, before * w[0:1, :], 0.0)
    tail = out[blk - sub:blk] + jnp.where(row == sub - 1, after * w[2:3, :], 0.0)
    return jnp.concatenate([head, out[sub:blk - sub], tail], axis=0)


def _conv_body(*refs, per_seq, glu, halo, blk):
    i = pl.program_id(1)
    first = (i % per_seq) == 0
    last = (i % per_seq) == per_seq - 1
    shift = refs[0][...]
    o_ref = refs[-1]
    tm = o_ref.shape[0]
    for lo in range(0, tm, blk):
        rows = slice(lo, lo + blk)
        if glu:
            ca, pa, na, wa, ba, cb, pb, nb, wb, bb = refs[1:-1]
            ua = _conv3(ca, pa, na, wa, ba, shift, first, last, halo, rows)
            ub = _conv3(cb, pb, nb, wb, bb, shift, first, last, halo, rows)
            o_ref[rows, :] = (_silu(ua) * ub).astype(o_ref.dtype)
        else:
            ca, pa, na, wa, ba = refs[1:-1]
            o_ref[rows, :] = _silu(_conv3(ca, pa, na, wa, ba, shift, first, last, halo, rows)).astype(o_ref.dtype)


def _dwconv_act(u, w, b, seq, glu):
    t, c_in = u.shape
    c_out = c_in // 2 if glu else c_in
    halo = V7X_BF16_SUBLANES
    tm = _pick(seq, (2048, 1024, 512, 256, 128, 64, 32, 16))
    tc = _pick(c_out, (512, 256, 128))
    blk = min(tm, V7X_MXU_DIM)
    per_seq = seq // tm
    hb = tm // halo
    n_halo = t // halo
    b2 = b.reshape(1, c_in)

    def group(off):
        return [pl.BlockSpec((tm, tc), lambda j, i: (i, j + off)),
                pl.BlockSpec((halo, tc), lambda j, i: (jnp.maximum(i * hb - 1, 0), j + off)),
                pl.BlockSpec((halo, tc), lambda j, i: (jnp.minimum((i + 1) * hb, n_halo - 1), j + off)),
                pl.BlockSpec((3, tc), lambda j, i: (0, j + off)),
                pl.BlockSpec((1, tc), lambda j, i: (0, j + off))]

    in_specs = [pl.BlockSpec((blk, 2 * blk), lambda j, i: (0, 0))] + group(0)
    args = [_shift_matrix(blk), u, u, u, w, b2]
    if glu:
        in_specs += group(c_out // tc)
        args += [u, u, u, w, b2]
    est = ((2 if glu else 1) * (2 * _nbytes((tm, tc), BF16) + 8 * _nbytes((blk, tc), F32))
           + 2 * _nbytes((tm, tc), BF16))
    return pl.pallas_call(
        functools.partial(_conv_body, per_seq=per_seq, glu=glu, halo=halo, blk=blk),
        grid=(c_out // tc, t // tm),
        in_specs=in_specs,
        out_specs=pl.BlockSpec((tm, tc), lambda j, i: (i, j)),
        out_shape=jax.ShapeDtypeStruct((t, c_out), BF16),
        compiler_params=_params(("parallel", "parallel"), est),
    )(*args)


def _split3_dot(tri, a):
    a1 = a.astype(BF16)
    r1 = a - a1.astype(F32)
    a2 = r1.astype(BF16)
    a3 = (r1 - a2.astype(F32)).astype(BF16)
    dot = functools.partial(jnp.dot, preferred_element_type=F32)
    return dot(tri, a1) + dot(tri, a2) + dot(tri, a3)


def _dt_body(raw_ref, bias_ref, alog_ref, cum_ref, src_ref, *, heads, groups):
    raw = raw_ref[...]
    v = raw + bias_ref[...]
    dt = jnp.maximum(v, 0.0) + jnp.log1p(jnp.exp(-jnp.abs(v)))
    a = dt * (-jnp.exp(alog_ref[...]))
    ln = raw.shape[0]
    r = lax.broadcasted_iota(jnp.int32, (ln, ln), 0)
    s = lax.broadcasted_iota(jnp.int32, (ln, ln), 1)
    lower = jnp.where(r >= s, 1.0, 0.0).astype(BF16)
    upper = jnp.where(r <= s, 1.0, 0.0).astype(BF16)
    cum = jnp.concatenate([_split3_dot(lower, a[:, :heads]), _split3_dot(upper, a[:, heads:])], axis=1)
    cum2 = cum * LOG2_E
    src_t = (cum2 - jnp.log(dt) * LOG2_E).T
    hg = heads // groups
    for d in range(2):
        for g in range(groups):
            lo = d * heads + g * hg
            cum_ref[d, g] = cum2[:, lo:lo + hg]
            src_ref[d, g, 0] = src_t[lo:lo + hg, :]


def _ssd_steps(dt_raw, dt_bias, a_log, groups):
    t, h2 = dt_raw.shape
    heads = h2 // 2
    hg = heads // groups
    ln = SSM_CHUNK
    nct = t // ln
    return pl.pallas_call(
        functools.partial(_dt_body, heads=heads, groups=groups),
        grid=(nct,),
        in_specs=[pl.BlockSpec((ln, h2), lambda c: (c, 0)),
                  pl.BlockSpec((1, h2), lambda c: (0, 0)),
                  pl.BlockSpec((1, h2), lambda c: (0, 0))],
        out_specs=[pl.BlockSpec((2, groups, ln, hg), lambda c: (0, 0, c, 0)),
                   pl.BlockSpec((2, groups, 1, hg, ln), lambda c: (0, 0, c, 0, 0))],
        out_shape=[jax.ShapeDtypeStruct((2, groups, t, hg), F32),
                   jax.ShapeDtypeStruct((2, groups, nct, hg, ln), F32)],
        compiler_params=_params(("parallel",), 0),
    )(dt_raw, dt_bias.reshape(1, h2), a_log.reshape(1, h2))


def _ssd_body(*refs, forward, groups, hg, p):
    if forward:
        x_ref, b_ref, c_ref, cum_ref, src_ref, yb_ref, z_ref, dsk_ref, w_ref, o_ref, h_ref, ch_ref = refs
    else:
        x_ref, b_ref, c_ref, cum_ref, src_ref, o_ref, h_ref, ch_ref = refs

    @pl.when(pl.program_id(1) == 0)
    def _():
        h_ref[...] = jnp.zeros_like(h_ref)

    ln = x_ref.shape[0]
    n = SSM_STATE
    width = 2 * p
    gw = hg * p
    r = lax.broadcasted_iota(jnp.int32, (ln, ln), 0)
    s = lax.broadcasted_iota(jnp.int32, (ln, ln), 1)
    mask = (r >= s) if forward else (r <= s)
    end_row = ln - 1 if forward else 0
    low = lax.broadcasted_iota(jnp.int32, (ln, width), 1) < p
    zero = jnp.zeros((ln, width), x_ref.dtype)

    def group(g, carry):
        gs = pl.ds(pl.multiple_of(g * gw, gw), gw)
        ns = pl.ds(pl.multiple_of(g * n, n), n)
        bm = b_ref[:, ns]
        cm = c_ref[:, ns]
        cb = lax.dot_general(cm, bm, (((1,), (1,)), ((), ())), preferred_element_type=F32)
        ch_ref[...] = jnp.dot(cm, h_ref[g].astype(BF16), preferred_element_type=F32)
        bm_t16 = bm.astype(F32).T.astype(BF16)
        cb16 = cb.astype(BF16)

        for i in range(hg // 2):
            ja, jb = 2 * i, 2 * i + 1
            sl = slice(i * width, (i + 1) * width)
            xs = pl.ds(pl.multiple_of(g * gw + i * width, width), width)
            cum_a = jnp.broadcast_to(cum_ref[g, :, ja:ja + 1], (ln, ln))
            cum_b = jnp.broadcast_to(cum_ref[g, :, jb:jb + 1], (ln, ln))
            src_a = src_ref[g, ja:ja + 1, :]
            src_b = src_ref[g, jb:jb + 1, :]
            m_a = cb16 * jnp.exp2(jnp.where(mask, cum_a - src_a, -jnp.inf)).astype(BF16)
            m_b = cb16 * jnp.exp2(jnp.where(mask, cum_b - src_b, -jnp.inf)).astype(BF16)
            xp = x_ref[:, xs]
            x_ab = jnp.concatenate([jnp.where(low, xp, zero), jnp.where(low, zero, xp)], axis=0)
            cum_e = jnp.where(low, cum_a, cum_b)
            y = (jnp.dot(jnp.concatenate([m_a, m_b], axis=1), x_ab, preferred_element_type=F32)
                 + jnp.exp2(cum_e) * ch_ref[:, sl])
            if forward:
                ch_ref[:, sl] = y
            else:
                o_ref[:, xs] = y.astype(o_ref.dtype)
            end_a = cum_ref[g, end_row:end_row + 1, ja:ja + 1]
            end_b = cum_ref[g, end_row:end_row + 1, jb:jb + 1]
            bt_ab = jnp.concatenate([bm_t16 * jnp.exp2(end_a - src_a).astype(BF16),
                                     bm_t16 * jnp.exp2(end_b - src_b).astype(BF16)], axis=1)
            h_ref[g, :, sl] = (h_ref[g, :, sl] * jnp.exp2(cum_e[end_row:end_row + 1, :])
                               + jnp.dot(bt_ab, x_ab, preferred_element_type=F32))

        if forward:
            y = ch_ref[...] + yb_ref[:, gs].astype(F32) + dsk_ref[:, gs] * x_ref[:, gs].astype(F32)
            yg = y * _silu(z_ref[:, gs].astype(F32))
            yg = yg * lax.rsqrt(jnp.mean(yg * yg, axis=-1, keepdims=True) + RMS_EPS)
            o_ref[:, gs] = (yg * w_ref[:, gs]).astype(o_ref.dtype)
        return carry

    lax.fori_loop(0, groups, group, 0, unroll=8)


def _ssd_scan(xbc_act, cum, src, bsz, seq, d_inner, groups, forward, epilogue_inputs=()):
    t = xbc_act.shape[0]
    n = SSM_STATE
    ln = SSM_CHUNK
    nc = seq // ln
    hg = cum.shape[3]
    heads = hg * groups
    p = d_inner // heads
    gw = hg * p
    gn = groups * n
    assert 2 * p == V7X_LANES and ln == V7X_LANES and hg % 2 == 0 and d_inner % gn == 0
    d = 0 if forward else 1

    def rb(b, c):
        return b * nc + (c if forward else nc - 1 - c)

    rows = pl.BlockSpec((ln, d_inner), lambda b, c: (rb(b, c), 0))
    vec = pl.BlockSpec((1, d_inner), lambda b, c: (0, 0))
    in_specs = [rows,
                pl.BlockSpec((ln, gn), lambda b, c: (rb(b, c), d_inner // gn)),
                pl.BlockSpec((ln, gn), lambda b, c: (rb(b, c), d_inner // gn + 1)),
                pl.BlockSpec((None, groups, ln, hg), lambda b, c: (d, 0, rb(b, c), 0)),
                pl.BlockSpec((None, groups, None, hg, ln), lambda b, c: (d, 0, rb(b, c), 0, 0))]
    args = [xbc_act, xbc_act, xbc_act, cum, src]
    if forward:
        y_bwd, z, d_skip_e, norm_w = epilogue_inputs
        in_specs += [rows, rows, vec, vec]
        args += [y_bwd, z, d_skip_e.reshape(1, d_inner), norm_w.reshape(1, d_inner)]
    est = (2 * (4 if forward else 2) * _nbytes((ln, d_inner), BF16) + 4 * _nbytes((ln, gn), BF16)
           + _nbytes((n, d_inner), F32) + 10 * _nbytes((ln, gw), F32) + 16 * _nbytes((ln, ln), F32))
    return pl.pallas_call(
        functools.partial(_ssd_body, forward=forward, groups=groups, hg=hg, p=p),
        grid=(bsz, nc),
        in_specs=in_specs,
        out_specs=rows,
        out_shape=jax.ShapeDtypeStruct((t, d_inner), BF16),
        scratch_shapes=[pltpu.VMEM((groups, n, gw), F32), pltpu.VMEM((ln, gw), F32)],
        compiler_params=_params(("parallel", "arbitrary"), est),
    )(*args)


def _rope_tables(seq, dh):
    half = dh // 2
    quarter = half // 2
    rows = seq // GRID_W
    row = jnp.repeat(jnp.arange(rows, dtype=jnp.int32), GRID_W).astype(F32)
    col = jnp.tile(jnp.arange(GRID_W, dtype=jnp.int32), rows).astype(F32)
    inv = 1.0 / (ROPE_THETA ** (jnp.arange(quarter, dtype=F32) / quarter))
    ang_r = row[:, None] * inv[None, :]
    ang_c = col[:, None] * inv[None, :]
    cos = jnp.concatenate([jnp.cos(ang_r), jnp.cos(ang_c), jnp.cos(ang_r), jnp.cos(ang_c)], axis=-1)
    sin = jnp.concatenate([-jnp.sin(ang_r), -jnp.sin(ang_c), jnp.sin(ang_r), jnp.sin(ang_c)], axis=-1)
    return cos, sin


def _pair_major(a, dh):
    lead = a.shape[:-1]
    nh = a.shape[-1] // dh
    a = a.reshape(lead + (nh, 2, 2, dh // 4))
    return jnp.swapaxes(a, -3, -2).reshape(lead + (nh * dh,))


def _norm_rope_epilogue(accs, extras, *, dh, scale):
    acc = accs[0]
    w, cos, sin = extras
    w = w * scale
    ones = jnp.ones((dh, dh), BF16)
    outs = []
    for hd in range(acc.shape[1] // dh):
        v = acc[:, hd * dh:(hd + 1) * dh]
        ss = jnp.dot((v * v).astype(BF16), ones, preferred_element_type=F32)
        v = v * lax.rsqrt(ss * (1.0 / dh) + RMS_EPS) * w
        outs.append(v * cos + pltpu.roll(v, dh // 2, 1) * sin)
    return [jnp.concatenate(outs, axis=1)]


def _proj_norm_rope(h, w, norm_w, cos, sin, seq, dh, scale, tm):
    n = w.shape[1]
    tn = _pick(n, (512, 256, 128))
    per_seq = seq // tm
    specs = [pl.BlockSpec((1, dh), lambda i, j: (0, 0)),
             pl.BlockSpec((tm, dh), lambda i, j: (i % per_seq, 0)),
             pl.BlockSpec((tm, dh), lambda i, j: (i % per_seq, 0))]
    (out,) = _fused_matmul([h], [_pair_major(w, dh)], [_pair_major(norm_w, dh).reshape(1, dh), cos, sin], specs,
                           functools.partial(_norm_rope_epilogue, dh=dh, scale=scale), [BF16], tm, tn)
    return out


def _attn_body(q_ref, k_ref, v_ref, o_ref, *, dh, kv_per_step):
    g_per_kv = q_ref.shape[1] // dh // kv_per_step
    for g in range(q_ref.shape[1] // dh):
        kv = g // g_per_kv
        k = k_ref[:, kv * dh:(kv + 1) * dh]
        v = v_ref[:, kv * dh:(kv + 1) * dh]
        sl = slice(g * dh, (g + 1) * dh)
        s = lax.dot_general(q_ref[:, sl], k, (((1,), (1,)), ((), ())), preferred_element_type=F32)
        e = jnp.exp2(s - jnp.max(s, axis=-1, keepdims=True))
        den = jnp.sum(e, axis=-1, keepdims=True)
        o = jnp.dot(e.astype(BF16), v, preferred_element_type=F32)
        o_ref[:, sl] = (o / den).astype(o_ref.dtype)


def _attention(q, k, v, bsz, seq, dh):
    t, q_cols = q.shape
    kv_heads = k.shape[1] // dh
    kv_per_step = 2 if kv_heads % 2 == 0 else 1
    gq = q_cols // kv_heads * kv_per_step
    tq = _pick(seq, (256, 128, 64, 32, 16))
    nq = seq // tq
    est = (4 * kv_per_step * _nbytes((seq, dh), BF16) + 4 * _nbytes((tq, gq), BF16)
           + 4 * _nbytes((tq, seq), F32) + 2 * _nbytes((tq, seq), BF16))
    return pl.pallas_call(
        functools.partial(_attn_body, dh=dh, kv_per_step=kv_per_step),
        grid=(bsz, kv_heads // kv_per_step, nq),
        in_specs=[pl.BlockSpec((tq, gq), lambda b, j, i: (b * nq + i, j)),
                  pl.BlockSpec((seq, kv_per_step * dh), lambda b, j, i: (b, j)),
                  pl.BlockSpec((seq, kv_per_step * dh), lambda b, j, i: (b, j))],
        out_specs=pl.BlockSpec((tq, gq), lambda b, j, i: (b * nq + i, j)),
        out_shape=jax.ShapeDtypeStruct((t, q_cols), BF16),
        compiler_params=_params(("parallel", "parallel", "arbitrary"), est),
    )(q, k, v)


def _plain(accs, extras):
    return [accs[0]]


def _gate_epilogue(accs, extras):
    return [_sigmoid(accs[0] + extras[0])]


def _mix_epilogue(accs, extras):
    return [extras[0].astype(F32) * accs[0] + extras[1].astype(F32) * accs[1]]


def kernel(x, c, w_ada, b_ada, w_in, ssm_conv_w, ssm_conv_b, ssm_a_log, ssm_dt_bias, ssm_d, ssm_norm_w,
           q_norm_w, k_norm_w, w_ssm_proj, w_attn_proj, w_gate, b_gate, w_out, ln1_g, ln1_b, w_up,
           ffn_conv_w, ffn_conv_b, w_down, ln2_g, ln2_b):
    bsz, seq, d = x.shape
    depth = w_ada.shape[0]
    t = bsz * seq
    d_inner = w_ssm_proj.shape[1]
    heads = ssm_a_log.shape[2]
    conv_dim = ssm_conv_w.shape[2]
    groups = (conv_dim - d_inner) // (2 * SSM_STATE)
    q_cols = w_attn_proj.shape[1]
    dh = q_norm_w.shape[1]
    kv_cols = (w_in.shape[2] - d_inner - conv_dim - 2 * heads - q_cols) // 2
    ffn = w_down.shape[1]
    alpha = (2.0 * depth) ** 0.25
    o_xbc = d_inner
    o_dt = o_xbc + conv_dim
    o_q = o_dt + 2 * heads
    o_k = o_q + q_cols
    o_v = o_k + kv_cols

    tm = _pick(seq, (1024, 512, 256, 128))
    cos, sin = _rope_tables(seq, dh)
    x2 = x.reshape(t, d)

    for layer in range(depth):
        wi = w_in[layer].astype(BF16)
        w_q = wi[:, o_q:o_k]
        w_k = wi[:, o_k:o_v]

        mod3 = _ada_mod(c, w_ada[layer], b_ada[layer]).reshape(bsz, 1, 6 * d)
        h = _ln_modulate(x2, mod3, seq, 0, 1)

        def proj(col0, width, dtype):
            tn = _pick(math.gcd(col0, width), (1024, 512, 256, 128))
            return _fused_matmul([h], [wi], [], [], _plain, [dtype], tm, tn, b_cols=(col0, width))[0]

        z = proj(0, o_xbc, BF16)
        xbc = proj(o_xbc, conv_dim, BF16)
        dt_raw = proj(o_dt, 2 * heads, F32)
        v = _fused_matmul([h], [wi[:, o_v:]], [], [], _plain, [BF16], tm,
                          _pick(kv_cols, (1024, 512, 256, 128)))[0]
        q = _proj_norm_rope(h, w_q, q_norm_w[layer], cos, sin, seq, dh, dh ** -0.5 * LOG2_E, tm)
        k = _proj_norm_rope(h, w_k, k_norm_w[layer], cos, sin, seq, dh, 1.0, tm)

        xbc_act = _dwconv_act(xbc, ssm_conv_w[layer], ssm_conv_b[layer], seq, glu=False)
        cum, src = _ssd_steps(dt_raw, ssm_dt_bias[layer], ssm_a_log[layer], groups)
        y_bwd = _ssd_scan(xbc_act, cum, src, bsz, seq, d_inner, groups, forward=False)
        d_skip_e = jnp.repeat(ssm_d[layer], d_inner // heads)
        y_ssm = _ssd_scan(xbc_act, cum, src, bsz, seq, d_inner, groups, forward=True,
                          epilogue_inputs=(y_bwd, z, d_skip_e, ssm_norm_w[layer]))

        y_attn = _attention(q, k, v, bsz, seq, dh)

        tn_d = _pick(d, (512, 256, 128))
        tn_wide = _pick(d, (1024, 512, 256, 128))
        wg = w_gate[layer].astype(BF16)
        bg = b_gate[layer].reshape(1, 2 * d)
        (gates,) = _fused_matmul([h], [wg], [bg], [pl.BlockSpec((1, tn_wide), lambda i, j: (0, j))],
                                 _gate_epilogue, [BF16], tm, tn_wide)
        nb = d // tn_d
        w_sp = w_ssm_proj[layer].astype(BF16)
        w_ap = w_attn_proj[layer].astype(BF16)
        tm_mix = _fit_tm(tm, tn_d, [y_ssm, y_attn], [w_sp, w_ap], 2, [BF16])
        (mixed_pre,) = _fused_matmul(
            [y_ssm, y_attn], [w_sp, w_ap], [gates, gates],
            [pl.BlockSpec((tm_mix, tn_d), lambda i, j: (i, j)),
             pl.BlockSpec((tm_mix, tn_d), lambda i, j: (i, j + nb))],
            _mix_epilogue, [BF16], tm_mix, tn_d)
        mixed = _fused_matmul([mixed_pre], [w_out[layer].astype(BF16)], [], [], _plain, [BF16], tm, tn_wide)[0]
        x2, h2 = _residual_ln(x2, mixed, mod3, 2, ln1_g[layer], ln1_b[layer], seq, alpha, h_idx=(3, 4))

        w_up_b = w_up[layer].astype(BF16)
        tn_up = _pick(2 * ffn, (1024, 512, 256, 128))
        tm_up = _fit_tm(_pick(seq, (2048, 1024, 512, 256, 128)), tn_up, [h2], [w_up_b], 0, [BF16])
        u = _fused_matmul([h2], [w_up_b], [], [], _plain, [BF16], tm_up, tn_up)[0]
        act = _dwconv_act(u, ffn_conv_w[layer], ffn_conv_b[layer], seq, glu=True)
        w_dn = w_down[layer].astype(BF16)
        tm_dn = _fit_tm(tm, tn_d, [act], [w_dn], 0, [BF16])
        f = _fused_matmul([act], [w_dn], [], [], _plain, [BF16], tm_dn, tn_d)[0]
        x2 = _residual_ln(x2, f, mod3, 5, ln2_g[layer], ln2_b[layer], seq, alpha)

    return x2.reshape(bsz, seq, d)
```

```python
import functools
import math

import jax
import jax.numpy as jnp
from jax import lax
from jax.experimental import pallas as pl
from jax.experimental.pallas import tpu as pltpu

F32 = jnp.float32
BF16 = jnp.bfloat16

GRID_W = 64
SSM_STATE = 128
SSM_CHUNK = 128
ROPE_THETA = 10000.0
LN_EPS = 1e-5
RMS_EPS = 1e-6
LOG2_E = 1.4426950408889634

V7X_LANES = 128
V7X_MXU_DIM = 256
V7X_F32_SUBLANES = 8
V7X_BF16_SUBLANES = 16
V7X_VMEM_BYTES = 64 * 1024 * 1024
VMEM_CEILING = V7X_VMEM_BYTES - 6 * 1024 * 1024


def _pick(n, cands):
    for c in cands:
        if n % c == 0:
            return c
    return n


def _params(sem, est_bytes):
    limit = int(min(max(est_bytes, 32 * 1024 * 1024), VMEM_CEILING))
    return pltpu.CompilerParams(dimension_semantics=sem, vmem_limit_bytes=limit)


def _nbytes(shape, dtype):
    n = 1
    for s in shape:
        n *= s
    return n * jnp.dtype(dtype).itemsize


def _sigmoid(v):
    return 1.0 / (1.0 + jnp.exp(-v))


def _silu(v):
    return v * _sigmoid(v)


def _layer_norm(v):
    mu = jnp.mean(v, axis=-1, keepdims=True)
    vc = v - mu
    var = jnp.mean(vc * vc, axis=-1, keepdims=True)
    return vc * lax.rsqrt(var + LN_EPS)


def _mm_body(*refs, n_pairs, n_extras, prologue, epilogue):
    a_refs = refs[:n_pairs]
    b_refs = refs[n_pairs:2 * n_pairs]
    e_refs = refs[2 * n_pairs:2 * n_pairs + n_extras]
    o_refs = refs[2 * n_pairs + n_extras:]
    accs = []
    for a_ref, b_ref in zip(a_refs, b_refs):
        a = a_ref[...]
        if prologue is not None:
            a = prologue(a)
        accs.append(jnp.dot(a, b_ref[...].astype(BF16), preferred_element_type=F32))
    outs = epilogue(accs, [e[...] for e in e_refs])
    for o_ref, v in zip(o_refs, outs):
        o_ref[...] = v.astype(o_ref.dtype)


def _mm_vmem(tm, tn, a_list, b_list, n_extras, out_dtypes, has_prologue):
    est = sum(2 * _nbytes((tm, a.shape[1]), a.dtype) for a in a_list)
    est += sum(2 * _nbytes((b.shape[0], tn), b.dtype) for b in b_list)
    est += n_extras * 2 * _nbytes((tm, tn), F32)
    est += sum(2 * _nbytes((tm, tn), dt) for dt in out_dtypes)
    est += (len(a_list) + 2) * _nbytes((tm, tn), F32)
    if has_prologue or any(b.dtype != BF16 for b in b_list):
        est += max(_nbytes((b.shape[0], tn), BF16) for b in b_list) + _nbytes((tm, a_list[0].shape[1]), F32)
    return est


def _fit_tm(tm, tn, a_list, b_list, n_extras, out_dtypes):
    while tm > V7X_BF16_SUBLANES and _mm_vmem(tm, tn, a_list, b_list, n_extras, out_dtypes, False) > VMEM_CEILING:
        tm //= 2
    return tm


def _fused_matmul(a_list, b_list, extras, extra_specs, epilogue, out_dtypes, tm, tn, prologue=None,
                  b_cols=None):
    m = a_list[0].shape[0]
    col0, n = b_cols if b_cols is not None else (0, b_list[0].shape[1])
    assert m % tm == 0 and n % tn == 0 and col0 % tn == 0
    off = col0 // tn
    in_specs = ([pl.BlockSpec((tm, a.shape[1]), lambda i, j: (i, 0)) for a in a_list]
                + [pl.BlockSpec((b.shape[0], tn), lambda i, j: (0, j + off)) for b in b_list]
                + list(extra_specs))
    out_specs = [pl.BlockSpec((tm, tn), lambda i, j: (i, j)) for _ in out_dtypes]
    est = _mm_vmem(tm, tn, a_list, b_list, len(extras), out_dtypes, prologue is not None)
    body = functools.partial(_mm_body, n_pairs=len(a_list), n_extras=len(extras),
                             prologue=prologue, epilogue=epilogue)
    outs = pl.pallas_call(
        body,
        grid=(m // tm, n // tn),
        in_specs=in_specs,
        out_specs=out_specs,
        out_shape=[jax.ShapeDtypeStruct((m, n), dt) for dt in out_dtypes],
        compiler_params=_params(("parallel", "arbitrary"), est),
    )(*a_list, *b_list, *extras)
    return outs


def _ada_mod(c, w_ada, b_ada):
    bsz, d = c.shape
    rows = 8
    c_pad = jnp.zeros((rows, d), F32).at[:bsz].set(c)
    n = w_ada.shape[1]
    tn = _pick(n, (512, 256, 128))

    def prologue(a):
        return _silu(a).astype(BF16)

    def epilogue(accs, extras):
        return [accs[0] + extras[0]]

    (mod,) = _fused_matmul([c_pad], [w_ada], [b_ada.reshape(1, n)],
                           [pl.BlockSpec((1, tn), lambda i, j: (0, j))],
                           epilogue, [F32], rows, tn, prologue=prologue)
    return mod[:bsz]


def _ln_mod_body(x_ref, sh_ref, sc_ref, o_ref):
    y = _layer_norm(x_ref[...])
    o_ref[...] = (y * (1.0 + sc_ref[...]) + sh_ref[...]).astype(o_ref.dtype)


def _ln_modulate(x2, mod3, seq, shift_idx, scale_idx):
    t, d = x2.shape
    tm = _pick(seq, (256, 128, 64, 32, 16))
    per_seq = seq // tm
    est = 2 * _nbytes((tm, d), F32) + 2 * _nbytes((tm, d), BF16) + 4 * _nbytes((tm, d), F32)
    return pl.pallas_call(
        _ln_mod_body,
        grid=(t // tm,),
        in_specs=[pl.BlockSpec((tm, d), lambda i: (i, 0)),
                  pl.BlockSpec((None, 1, d), lambda i: (i // per_seq, 0, shift_idx)),
                  pl.BlockSpec((None, 1, d), lambda i: (i // per_seq, 0, scale_idx))],
        out_specs=pl.BlockSpec((tm, d), lambda i: (i, 0)),
        out_shape=jax.ShapeDtypeStruct((t, d), BF16),
        compiler_params=_params(("parallel",), est),
    )(x2, mod3, mod3)


def _res_ln_body(*refs, alpha, with_h):
    if with_h:
        x_ref, m_ref, gate_ref, g_ref, b_ref, sh_ref, sc_ref, o_ref, h_ref = refs
    else:
        x_ref, m_ref, gate_ref, g_ref, b_ref, o_ref = refs
    v = alpha * x_ref[...] + gate_ref[...] * m_ref[...].astype(F32)
    xn = _layer_norm(v) * g_ref[...] + b_ref[...]
    o_ref[...] = xn
    if with_h:
        h_ref[...] = (_layer_norm(xn) * (1.0 + sc_ref[...]) + sh_ref[...]).astype(h_ref.dtype)


def _residual_ln(x2, m2, mod3, gate_idx, ln_g, ln_b, seq, alpha, h_idx=None):
    t, d = x2.shape
    tm = _pick(seq, (256, 128, 64, 32, 16))
    per_seq = seq // tm
    with_h = h_idx is not None
    row = pl.BlockSpec((tm, d), lambda i: (i, 0))
    vec = pl.BlockSpec((1, d), lambda i: (0, 0))

    def modspec(k):
        return pl.BlockSpec((None, 1, d), lambda i: (i // per_seq, 0, k))

    in_specs = [row, row, modspec(gate_idx), vec, vec]
    args = [x2, m2, mod3, ln_g.reshape(1, d), ln_b.reshape(1, d)]
    out_specs = [row]
    out_shape = [jax.ShapeDtypeStruct((t, d), F32)]
    if with_h:
        in_specs += [modspec(h_idx[0]), modspec(h_idx[1])]
        args += [mod3, mod3]
        out_specs.append(row)
        out_shape.append(jax.ShapeDtypeStruct((t, d), BF16))
    est = 12 * _nbytes((tm, d), F32)
    outs = pl.pallas_call(
        functools.partial(_res_ln_body, alpha=alpha, with_h=with_h),
        grid=(t // tm,),
        in_specs=in_specs,
        out_specs=out_specs,
        out_shape=out_shape,
        compiler_params=_params(("parallel",), est),
    )(*args)
    return outs if with_h else outs[0]


def _shift_matrix(blk):
    i = jnp.arange(blk)[:, None]
    j = jnp.arange(blk)[None, :]
    return jnp.concatenate([j == i - 1, j == i + 1], axis=1).astype(BF16)


def _conv3(cur_ref, prev_ref, next_ref, w_ref, b_ref, shift, first, last, halo, rows):
    tm = cur_ref.shape[0]
    lo, hi = rows.start, rows.stop
    blk = hi - lo
    sub = V7X_F32_SUBLANES
    w = w_ref[...]
    w_side = w.astype(BF16)
    xb = cur_ref[lo:hi, :]
    side = jnp.concatenate([xb * w_side[0:1, :], xb * w_side[2:3, :]], axis=0)
    out = (jnp.dot(shift, side, preferred_element_type=F32)
           + ((xb * w_side[1:2, :]).astype(F32) + b_ref[...]))
    if lo == 0:
        before = prev_ref[halo - 1:halo, :].astype(F32) * jnp.where(first, 0.0, 1.0)
    else:
        before = cur_ref[lo - 1:lo, :].astype(F32)
    if hi == tm:
        after = next_ref[0:1, :].astype(F32) * jnp.where(last, 0.0, 1.0)
    else:
        after = cur_ref[hi:hi + 1, :].astype(F32)
    row = lax.broadcasted_iota(jnp.int32, (sub, xb.shape[1]), 0)
    head = out[0:sub] + jnp.where(row == 0, before * w[0:1, :], 0.0)
    tail = out[blk - sub:blk] + jnp.where(row == sub - 1, after * w[2:3, :], 0.0)
    return jnp.concatenate([head, out[sub:blk - sub], tail], axis=0)


def _conv_body(*refs, per_seq, glu, halo, blk):
    i = pl.program_id(1)
    first = (i % per_seq) == 0
    last = (i % per_seq) == per_seq - 1
    shift = refs[0][...]
    o_ref = refs[-1]
    tm = o_ref.shape[0]
    for lo in range(0, tm, blk):
        rows = slice(lo, lo + blk)
        if glu:
            ca, pa, na, wa, ba, cb, pb, nb, wb, bb = refs[1:-1]
            ua = _conv3(ca, pa, na, wa, ba, shift, first, last, halo, rows)
            ub = _conv3(cb, pb, nb, wb, bb, shift, first, last, halo, rows)
            o_ref[rows, :] = (_silu(ua) * ub).astype(o_ref.dtype)
        else:
            ca, pa, na, wa, ba = refs[1:-1]
            o_ref[rows, :] = _silu(_conv3(ca, pa, na, wa, ba, shift, first, last, halo, rows)).astype(o_ref.dtype)


def _dwconv_act(u, w, b, seq, glu):
    t, c_in = u.shape
    c_out = c_in // 2 if glu else c_in
    halo = V7X_BF16_SUBLANES
    tm = _pick(seq, (2048, 1024, 512, 256, 128, 64, 32, 16))
    tc = _pick(c_out, (512, 256, 128))
    blk = min(tm, V7X_MXU_DIM)
    per_seq = seq // tm
    hb = tm // halo
    n_halo = t // halo
    b2 = b.reshape(1, c_in)

    def group(off):
        return [pl.BlockSpec((tm, tc), lambda j, i: (i, j + off)),
                pl.BlockSpec((halo, tc), lambda j, i: (jnp.maximum(i * hb - 1, 0), j + off)),
                pl.BlockSpec((halo, tc), lambda j, i: (jnp.minimum((i + 1) * hb, n_halo - 1), j + off)),
                pl.BlockSpec((3, tc), lambda j, i: (0, j + off)),
                pl.BlockSpec((1, tc), lambda j, i: (0, j + off))]

    in_specs = [pl.BlockSpec((blk, 2 * blk), lambda j, i: (0, 0))] + group(0)
    args = [_shift_matrix(blk), u, u, u, w, b2]
    if glu:
        in_specs += group(c_out // tc)
        args += [u, u, u, w, b2]
    est = ((2 if glu else 1) * (2 * _nbytes((tm, tc), BF16) + 8 * _nbytes((blk, tc), F32))
           + 2 * _nbytes((tm, tc), BF16))
    return pl.pallas_call(
        functools.partial(_conv_body, per_seq=per_seq, glu=glu, halo=halo, blk=blk),
        grid=(c_out // tc, t // tm),
        in_specs=in_specs,
        out_specs=pl.BlockSpec((tm, tc), lambda j, i: (i, j)),
        out_shape=jax.ShapeDtypeStruct((t, c_out), BF16),
        compiler_params=_params(("parallel", "parallel"), est),
    )(*args)


def _split3_dot(tri, a):
    a1 = a.astype(BF16)
    r1 = a - a1.astype(F32)
    a2 = r1.astype(BF16)
    a3 = (r1 - a2.astype(F32)).astype(BF16)
    dot = functools.partial(jnp.dot, preferred_element_type=F32)
    return dot(tri, a1) + dot(tri, a2) + dot(tri, a3)


def _dt_body(raw_ref, bias_ref, alog_ref, cum_ref, src_ref, *, heads, groups):
    raw = raw_ref[...]
    v = raw + bias_ref[...]
    dt = jnp.maximum(v, 0.0) + jnp.log1p(jnp.exp(-jnp.abs(v)))
    a = dt * (-jnp.exp(alog_ref[...]))
    ln = raw.shape[0]
    r = lax.broadcasted_iota(jnp.int32, (ln, ln), 0)
    s = lax.broadcasted_iota(jnp.int32, (ln, ln), 1)
    lower = jnp.where(r >= s, 1.0, 0.0).astype(BF16)
    upper = jnp.where(r <= s, 1.0, 0.0).astype(BF16)
    cum = jnp.concatenate([_split3_dot(lower, a[:, :heads]), _split3_dot(upper, a[:, heads:])], axis=1)
    cum2 = cum * LOG2_E
    src_t = (cum2 - jnp.log(dt) * LOG2_E).T
    hg = heads // groups
    for d in range(2):
        for g in range(groups):
            lo = d * heads + g * hg
            cum_ref[d, g] = cum2[:, lo:lo + hg]
            src_ref[d, g, 0] = src_t[lo:lo + hg, :]


def _ssd_steps(dt_raw, dt_bias, a_log, groups):
    t, h2 = dt_raw.shape
    heads = h2 // 2
    hg = heads // groups
    ln = SSM_CHUNK
    nct = t // ln
    return pl.pallas_call(
        functools.partial(_dt_body, heads=heads, groups=groups),
        grid=(nct,),
        in_specs=[pl.BlockSpec((ln, h2), lambda c: (c, 0)),
                  pl.BlockSpec((1, h2), lambda c: (0, 0)),
                  pl.BlockSpec((1, h2), lambda c: (0, 0))],
        out_specs=[pl.BlockSpec((2, groups, ln, hg), lambda c: (0, 0, c, 0)),
                   pl.BlockSpec((2, groups, 1, hg, ln), lambda c: (0, 0, c, 0, 0))],
        out_shape=[jax.ShapeDtypeStruct((2, groups, t, hg), F32),
                   jax.ShapeDtypeStruct((2, groups, nct, hg, ln), F32)],
        compiler_params=_params(("parallel",), 0),
    )(dt_raw, dt_bias.reshape(1, h2), a_log.reshape(1, h2))


def _ssd_body(*refs, forward, groups, hg, p):
    if forward:
        x_ref, b_ref, c_ref, cum_ref, src_ref, yb_ref, z_ref, dsk_ref, w_ref, o_ref, h_ref, ch_ref = refs
    else:
        x_ref, b_ref, c_ref, cum_ref, src_ref, o_ref, h_ref, ch_ref = refs

    @pl.when(pl.program_id(1) == 0)
    def _():
        h_ref[...] = jnp.zeros_like(h_ref)

    ln = x_ref.shape[0]
    n = SSM_STATE
    width = 2 * p
    gw = hg * p
    r = lax.broadcasted_iota(jnp.int32, (ln, ln), 0)
    s = lax.broadcasted_iota(jnp.int32, (ln, ln), 1)
    mask = (r >= s) if forward else (r <= s)
    end_row = ln - 1 if forward else 0
    low = lax.broadcasted_iota(jnp.int32, (ln, width), 1) < p
    zero = jnp.zeros((ln, width), x_ref.dtype)

    def group(g, carry):
        gs = pl.ds(pl.multiple_of(g * gw, gw), gw)
        ns = pl.ds(pl.multiple_of(g * n, n), n)
        bm = b_ref[:, ns]
        cm = c_ref[:, ns]
        cb = lax.dot_general(cm, bm, (((1,), (1,)), ((), ())), preferred_element_type=F32)
        ch_ref[...] = jnp.dot(cm, h_ref[g].astype(BF16), preferred_element_type=F32)
        bm_t16 = bm.astype(F32).T.astype(BF16)
        cb16 = cb.astype(BF16)

        for i in range(hg // 2):
            ja, jb = 2 * i, 2 * i + 1
            sl = slice(i * width, (i + 1) * width)
            xs = pl.ds(pl.multiple_of(g * gw + i * width, width), width)
            cum_a = jnp.broadcast_to(cum_ref[g, :, ja:ja + 1], (ln, ln))
            cum_b = jnp.broadcast_to(cum_ref[g, :, jb:jb + 1], (ln, ln))
            src_a = src_ref[g, ja:ja + 1, :]
            src_b = src_ref[g, jb:jb + 1, :]
            m_a = cb16 * jnp.exp2(jnp.where(mask, cum_a - src_a, -jnp.inf)).astype(BF16)
            m_b = cb16 * jnp.exp2(jnp.where(mask, cum_b - src_b, -jnp.inf)).astype(BF16)
            xp = x_ref[:, xs]
            x_ab = jnp.concatenate([jnp.where(low, xp, zero), jnp.where(low, zero, xp)], axis=0)
            cum_e = jnp.where(low, cum_a, cum_b)
            y = (jnp.dot(jnp.concatenate([m_a, m_b], axis=1), x_ab, preferred_element_type=F32)
                 + jnp.exp2(cum_e) * ch_ref[:, sl])
            if forward:
                ch_ref[:, sl] = y
            else:
                o_ref[:, xs] = y.astype(o_ref.dtype)
            end_a = cum_ref[g, end_row:end_row + 1, ja:ja + 1]
            end_b = cum_ref[g, end_row:end_row + 1, jb:jb + 1]
            bt_ab = jnp.concatenate([bm_t16 * jnp.exp2(end_a - src_a).astype(BF16),
                                     bm_t16 * jnp.exp2(end_b - src_b).astype(BF16)], axis=1)
            h_ref[g, :, sl] = (h_ref[g, :, sl] * jnp.exp2(cum_e[end_row:end_row + 1, :])
                               + jnp.dot(bt_ab, x_ab, preferred_element_type=F32))

        if forward:
            y = ch_ref[...] + yb_ref[:, gs].astype(F32) + dsk_ref[:, gs] * x_ref[:, gs].astype(F32)
            yg = y * _silu(z_ref[:, gs].astype(F32))
            yg = yg * lax.rsqrt(jnp.mean(yg * yg, axis=-1, keepdims=True) + RMS_EPS)
            o_ref[:, gs] = (yg * w_ref[:, gs]).astype(o_ref.dtype)
        return carry

    lax.fori_loop(0, groups, group, 0, unroll=8)


def _ssd_scan(xbc_act, cum, src, bsz, seq, d_inner, groups, forward, epilogue_inputs=()):
    t = xbc_act.shape[0]
    n = SSM_STATE
    ln = SSM_CHUNK
    nc = seq // ln
    hg = cum.shape[3]
    heads = hg * groups
    p = d_inner // heads
    gw = hg * p
    gn = groups * n
    assert 2 * p == V7X_LANES and ln == V7X_LANES and hg % 2 == 0 and d_inner % gn == 0
    d = 0 if forward else 1

    def rb(b, c):
        return b * nc + (c if forward else nc - 1 - c)

    rows = pl.BlockSpec((ln, d_inner), lambda b, c: (rb(b, c), 0))
    vec = pl.BlockSpec((1, d_inner), lambda b, c: (0, 0))
    in_specs = [rows,
                pl.BlockSpec((ln, gn), lambda b, c: (rb(b, c), d_inner // gn)),
                pl.BlockSpec((ln, gn), lambda b, c: (rb(b, c), d_inner // gn + 1)),
                pl.BlockSpec((None, groups, ln, hg), lambda b, c: (d, 0, rb(b, c), 0)),
                pl.BlockSpec((None, groups, None, hg, ln), lambda b, c: (d, 0, rb(b, c), 0, 0))]
    args = [xbc_act, xbc_act, xbc_act, cum, src]
    if forward:
        y_bwd, z, d_skip_e, norm_w = epilogue_inputs
        in_specs += [rows, rows, vec, vec]
        args += [y_bwd, z, d_skip_e.reshape(1, d_inner), norm_w.reshape(1, d_inner)]
    est = (2 * (4 if forward else 2) * _nbytes((ln, d_inner), BF16) + 4 * _nbytes((ln, gn), BF16)
           + _nbytes((n, d_inner), F32) + 10 * _nbytes((ln, gw), F32) + 16 * _nbytes((ln, ln), F32))
    return pl.pallas_call(
        functools.partial(_ssd_body, forward=forward, groups=groups, hg=hg, p=p),
        grid=(bsz, nc),
        in_specs=in_specs,
        out_specs=rows,
        out_shape=jax.ShapeDtypeStruct((t, d_inner), BF16),
        scratch_shapes=[pltpu.VMEM((groups, n, gw), F32), pltpu.VMEM((ln, gw), F32)],
        compiler_params=_params(("parallel", "arbitrary"), est),
    )(*args)


def _rope_tables(seq, dh):
    half = dh // 2
    quarter = half // 2
    rows = seq // GRID_W
    row = jnp.repeat(jnp.arange(rows, dtype=jnp.int32), GRID_W).astype(F32)
    col = jnp.tile(jnp.arange(GRID_W, dtype=jnp.int32), rows).astype(F32)
    inv = 1.0 / (ROPE_THETA ** (jnp.arange(quarter, dtype=F32) / quarter))
    ang_r = row[:, None] * inv[None, :]
    ang_c = col[:, None] * inv[None, :]
    cos = jnp.concatenate([jnp.cos(ang_r), jnp.cos(ang_c), jnp.cos(ang_r), jnp.cos(ang_c)], axis=-1)
    sin = jnp.concatenate([-jnp.sin(ang_r), -jnp.sin(ang_c), jnp.sin(ang_r), jnp.sin(ang_c)], axis=-1)
    return cos, sin


def _pair_major(a, dh):
    lead = a.shape[:-1]
    nh = a.shape[-1] // dh
    a = a.reshape(lead + (nh, 2, 2, dh // 4))
    return jnp.swapaxes(a, -3, -2).reshape(lead + (nh * dh,))


def _norm_rope_epilogue(accs, extras, *, dh, scale):
    acc = accs[0]
    w, cos, sin = extras
    w = w * scale
    ones = jnp.ones((dh, dh), BF16)
    outs = []
    for hd in range(acc.shape[1] // dh):
        v = acc[:, hd * dh:(hd + 1) * dh]
        ss = jnp.dot((v * v).astype(BF16), ones, preferred_element_type=F32)
        v = v * lax.rsqrt(ss * (1.0 / dh) + RMS_EPS) * w
        outs.append(v * cos + pltpu.roll(v, dh // 2, 1) * sin)
    return [jnp.concatenate(outs, axis=1)]


def _proj_norm_rope(h, w, norm_w, cos, sin, seq, dh, scale, tm):
    n = w.shape[1]
    tn = _pick(n, (512, 256, 128))
    per_seq = seq // tm
    specs = [pl.BlockSpec((1, dh), lambda i, j: (0, 0)),
             pl.BlockSpec((tm, dh), lambda i, j: (i % per_seq, 0)),
             pl.BlockSpec((tm, dh), lambda i, j: (i % per_seq, 0))]
    (out,) = _fused_matmul([h], [_pair_major(w, dh)], [_pair_major(norm_w, dh).reshape(1, dh), cos, sin], specs,
                           functools.partial(_norm_rope_epilogue, dh=dh, scale=scale), [BF16], tm, tn)
    return out


def _attn_body(q_ref, k_ref, v_ref, o_ref, *, dh, kv_per_step):
    g_per_kv = q_ref.shape[1] // dh // kv_per_step
    for g in range(q_ref.shape[1] // dh):
        kv = g // g_per_kv
        k = k_ref[:, kv * dh:(kv + 1) * dh]
        v = v_ref[:, kv * dh:(kv + 1) * dh]
        sl = slice(g * dh, (g + 1) * dh)
        s = lax.dot_general(q_ref[:, sl], k, (((1,), (1,)), ((), ())), preferred_element_type=F32)
        e = jnp.exp2(s - jnp.max(s, axis=-1, keepdims=True))
        den = jnp.sum(e, axis=-1, keepdims=True)
        o = jnp.dot(e.astype(BF16), v, preferred_element_type=F32)
        o_ref[:, sl] = (o / den).astype(o_ref.dtype)


def _attention(q, k, v, bsz, seq, dh):
    t, q_cols = q.shape
    kv_heads = k.shape[1] // dh
    kv_per_step = 2 if kv_heads % 2 == 0 else 1
    gq = q_cols // kv_heads * kv_per_step
    tq = _pick(seq, (256, 128, 64, 32, 16))
    nq = seq // tq
    est = (4 * kv_per_step * _nbytes((seq, dh), BF16) + 4 * _nbytes((tq, gq), BF16)
           + 4 * _nbytes((tq, seq), F32) + 2 * _nbytes((tq, seq), BF16))
    return pl.pallas_call(
        functools.partial(_attn_body, dh=dh, kv_per_step=kv_per_step),
        grid=(bsz, kv_heads // kv_per_step, nq),
        in_specs=[pl.BlockSpec((tq, gq), lambda b, j, i: (b * nq + i, j)),
                  pl.BlockSpec((seq, kv_per_step * dh), lambda b, j, i: (b, j)),
                  pl.BlockSpec((seq, kv_per_step * dh), lambda b, j, i: (b, j))],
        out_specs=pl.BlockSpec((tq, gq), lambda b, j, i: (b * nq + i, j)),
        out_shape=jax.ShapeDtypeStruct((t, q_cols), BF16),
        compiler_params=_params(("parallel", "parallel", "arbitrary"), est),
    )(q, k, v)


def _plain(accs, extras):
    return [accs[0]]


def _gate_epilogue(accs, extras):
    return [_sigmoid(accs[0] + extras[0])]


def _mix_epilogue(accs, extras):
    return [extras[0].astype(F32) * accs[0] + extras[1].astype(F32) * accs[1]]


def kernel(x, c, w_ada, b_ada, w_in, ssm_conv_w, ssm_conv_b, ssm_a_log, ssm_dt_bias, ssm_d, ssm_norm_w,
           q_norm_w, k_norm_w, w_ssm_proj, w_attn_proj, w_gate, b_gate, w_out, ln1_g, ln1_b, w_up,
           ffn_conv_w, ffn_conv_b, w_down, ln2_g, ln2_b):
    bsz, seq, d = x.shape
    depth = w_ada.shape[0]
    t = bsz * seq
    d_inner = w_ssm_proj.shape[1]
    heads = ssm_a_log.shape[2]
    conv_dim = ssm_conv_w.shape[2]
    groups = (conv_dim - d_inner) // (2 * SSM_STATE)
    q_cols = w_attn_proj.shape[1]
    dh = q_norm_w.shape[1]
    kv_cols = (w_in.shape[2] - d_inner - conv_dim - 2 * heads - q_cols) // 2
    ffn = w_down.shape[1]
    alpha = (2.0 * depth) ** 0.25
    o_xbc = d_inner
    o_dt = o_xbc + conv_dim
    o_q = o_dt + 2 * heads
    o_k = o_q + q_cols
    o_v = o_k + kv_cols

    tm = _pick(seq, (1024, 512, 256, 128))
    cos, sin = _rope_tables(seq, dh)
    x2 = x.reshape(t, d)

    for layer in range(depth):
        wi = w_in[layer].astype(BF16)
        w_q = wi[:, o_q:o_k]
        w_k = wi[:, o_k:o_v]

        mod3 = _ada_mod(c, w_ada[layer], b_ada[layer]).reshape(bsz, 1, 6 * d)
        h = _ln_modulate(x2, mod3, seq, 0, 1)

        def proj(col0, width, dtype):
            tn = _pick(math.gcd(col0, width), (1024, 512, 256, 128))
            return _fused_matmul([h], [wi], [], [], _plain, [dtype], tm, tn, b_cols=(col0, width))[0]

        z = proj(0, o_xbc, BF16)
        xbc = proj(o_xbc, conv_dim, BF16)
        dt_raw = proj(o_dt, 2 * heads, F32)
        v = _fused_matmul([h], [wi[:, o_v:]], [], [], _plain, [BF16], tm,
                          _pick(kv_cols, (1024, 512, 256, 128)))[0]
        q = _proj_norm_rope(h, w_q, q_norm_w[layer], cos, sin, seq, dh, dh ** -0.5 * LOG2_E, tm)
        k = _proj_norm_rope(h, w_k, k_norm_w[layer], cos, sin, seq, dh, 1.0, tm)

        xbc_act = _dwconv_act(xbc, ssm_conv_w[layer], ssm_conv_b[layer], seq, glu=False)
        cum, src = _ssd_steps(dt_raw, ssm_dt_bias[layer], ssm_a_log[layer], groups)
        y_bwd = _ssd_scan(xbc_act, cum, src, bsz, seq, d_inner, groups, forward=False)
        d_skip_e = jnp.repeat(ssm_d[layer], d_inner // heads)
        y_ssm = _ssd_scan(xbc_act, cum, src, bsz, seq, d_inner, groups, forward=True,
                          epilogue_inputs=(y_bwd, z, d_skip_e, ssm_norm_w[layer]))

        y_attn = _attention(q, k, v, bsz, seq, dh)

        tn_d = _pick(d, (512, 256, 128))
        tn_wide = _pick(d, (1024, 512, 256, 128))
        wg = w_gate[layer].astype(BF16)
        bg = b_gate[layer].reshape(1, 2 * d)
        (gates,) = _fused_matmul([h], [wg], [bg], [pl.BlockSpec((1, tn_wide), lambda i, j: (0, j))],
                                 _gate_epilogue, [BF16], tm, tn_wide)
        nb = d // tn_d
        w_sp = w_ssm_proj[layer].astype(BF16)
        w_ap = w_attn_proj[layer].astype(BF16)
        tm_mix = _fit_tm(tm, tn_d, [y_ssm, y_attn], [w_sp, w_ap], 2, [BF16])
        (mixed_pre,) = _fused_matmul(
            [y_ssm, y_attn], [w_sp, w_ap], [gates, gates],
            [pl.BlockSpec((tm_mix, tn_d), lambda i, j: (i, j)),
             pl.BlockSpec((tm_mix, tn_d), lambda i, j: (i, j + nb))],
            _mix_epilogue, [BF16], tm_mix, tn_d)
        mixed = _fused_matmul([mixed_pre], [w_out[layer].astype(BF16)], [], [], _plain, [BF16], tm, tn_wide)[0]
        x2, h2 = _residual_ln(x2, mixed, mod3, 2, ln1_g[layer], ln1_b[layer], seq, alpha, h_idx=(3, 4))

        w_up_b = w_up[layer].astype(BF16)
        tn_up = _pick(2 * ffn, (1024, 512, 256, 128))
        tm_up = _fit_tm(_pick(seq, (2048, 1024, 512, 256, 128)), tn_up, [h2], [w_up_b], 0, [BF16])
        u = _fused_matmul([h2], [w_up_b], [], [], _plain, [BF16], tm_up, tn_up)[0]
        act = _dwconv_act(u, ffn_conv_w[layer], ffn_conv_b[layer], seq, glu=True)
        w_dn = w_down[layer].astype(BF16)
        tm_dn = _fit_tm(tm, tn_d, [act], [w_dn], 0, [BF16])
        f = _fused_matmul([act], [w_dn], [], [], _plain, [BF16], tm_dn, tn_d)[0]
        x2 = _residual_ln(x2, f, mod3, 5, ln2_g[layer], ln2_b[layer], seq, alpha)

    return x2.reshape(bsz, seq, d)
```

```python
import functools
import math

import jax
import jax.numpy as jnp
from jax import lax
from jax.experimental import pallas as pl
from jax.experimental.pallas import tpu as pltpu

F32 = jnp.float32
BF16 = jnp.bfloat16

GRID_W = 64
SSM_STATE = 128
SSM_CHUNK = 128
ROPE_THETA = 10000.0
LN_EPS = 1e-5
RMS_EPS = 1e-6
LOG2_E = 1.4426950408889634

V7X_LANES = 128
V7X_MXU_DIM = 256
V7X_F32_SUBLANES = 8
V7X_BF16_SUBLANES = 16
V7X_VMEM_BYTES = 64 * 1024 * 1024
VMEM_CEILING = V7X_VMEM_BYTES - 6 * 1024 * 1024


def _pick(n, cands):
    for c in cands:
        if n % c == 0:
            return c
    return n


def _params(sem, est_bytes):
    limit = int(min(max(est_bytes, 32 * 1024 * 1024), VMEM_CEILING))
    return pltpu.CompilerParams(dimension_semantics=sem, vmem_limit_bytes=limit)


def _nbytes(shape, dtype):
    n = 1
    for s in shape:
        n *= s
    return n * jnp.dtype(dtype).itemsize


def _sigmoid(v):
    return 1.0 / (1.0 + jnp.exp(-v))


def _silu(v):
    return v * _sigmoid(v)


def _layer_norm(v):
    mu = jnp.mean(v, axis=-1, keepdims=True)
    vc = v - mu
    var = jnp.mean(vc * vc, axis=-1, keepdims=True)
    return vc * lax.rsqrt(var + LN_EPS)


def _mm_body(*refs, n_pairs, n_extras, prologue, epilogue):
    a_refs = refs[:n_pairs]
    b_refs = refs[n_pairs:2 * n_pairs]
    e_refs = refs[2 * n_pairs:2 * n_pairs + n_extras]
    o_refs = refs[2 * n_pairs + n_extras:]
    accs = []
    for a_ref, b_ref in zip(a_refs, b_refs):
        a = a_ref[...]
        if prologue is not None:
            a = prologue(a)
        accs.append(jnp.dot(a, b_ref[...].astype(BF16), preferred_element_type=F32))
    outs = epilogue(accs, [e[...] for e in e_refs])
    for o_ref, v in zip(o_refs, outs):
        o_ref[...] = v.astype(o_ref.dtype)


def _mm_vmem(tm, tn, a_list, b_list, n_extras, out_dtypes, has_prologue):
    est = sum(2 * _nbytes((tm, a.shape[1]), a.dtype) for a in a_list)
    est += sum(2 * _nbytes((b.shape[0], tn), b.dtype) for b in b_list)
    est += n_extras * 2 * _nbytes((tm, tn), F32)
    est += sum(2 * _nbytes((tm, tn), dt) for dt in out_dtypes)
    est += (len(a_list) + 2) * _nbytes((tm, tn), F32)
    if has_prologue or any(b.dtype != BF16 for b in b_list):
        est += max(_nbytes((b.shape[0], tn), BF16) for b in b_list) + _nbytes((tm, a_list[0].shape[1]), F32)
    return est


def _fit_tm(tm, tn, a_list, b_list, n_extras, out_dtypes):
    while tm > V7X_BF16_SUBLANES and _mm_vmem(tm, tn, a_list, b_list, n_extras, out_dtypes, False) > VMEM_CEILING:
        tm //= 2
    return tm


def _fused_matmul(a_list, b_list, extras, extra_specs, epilogue, out_dtypes, tm, tn, prologue=None,
                  b_cols=None):
    m = a_list[0].shape[0]
    col0, n = b_cols if b_cols is not None else (0, b_list[0].shape[1])
    assert m % tm == 0 and n % tn == 0 and col0 % tn == 0
    off = col0 // tn
    in_specs = ([pl.BlockSpec((tm, a.shape[1]), lambda i, j: (i, 0)) for a in a_list]
                + [pl.BlockSpec((b.shape[0], tn), lambda i, j: (0, j + off)) for b in b_list]
                + list(extra_specs))
    out_specs = [pl.BlockSpec((tm, tn), lambda i, j: (i, j)) for _ in out_dtypes]
    est = _mm_vmem(tm, tn, a_list, b_list, len(extras), out_dtypes, prologue is not None)
    body = functools.partial(_mm_body, n_pairs=len(a_list), n_extras=len(extras),
                             prologue=prologue, epilogue=epilogue)
    outs = pl.pallas_call(
        body,
        grid=(m // tm, n // tn),
        in_specs=in_specs,
        out_specs=out_specs,
        out_shape=[jax.ShapeDtypeStruct((m, n), dt) for dt in out_dtypes],
        compiler_params=_params(("parallel", "arbitrary"), est),
    )(*a_list, *b_list, *extras)
    return outs


def _ada_mod(c, w_ada, b_ada):
    bsz, d = c.shape
    rows = 8
    c_pad = jnp.zeros((rows, d), F32).at[:bsz].set(c)
    n = w_ada.shape[1]
    tn = _pick(n, (512, 256, 128))

    def prologue(a):
        return _silu(a).astype(BF16)

    def epilogue(accs, extras):
        return [accs[0] + extras[0]]

    (mod,) = _fused_matmul([c_pad], [w_ada], [b_ada.reshape(1, n)],
                           [pl.BlockSpec((1, tn), lambda i, j: (0, j))],
                           epilogue, [F32], rows, tn, prologue=prologue)
    return mod[:bsz]


def _ln_mod_body(x_ref, sh_ref, sc_ref, o_ref):
    y = _layer_norm(x_ref[...])
    o_ref[...] = (y * (1.0 + sc_ref[...]) + sh_ref[...]).astype(o_ref.dtype)


def _ln_modulate(x2, mod3, seq, shift_idx, scale_idx):
    t, d = x2.shape
    tm = _pick(seq, (256, 128, 64, 32, 16))
    per_seq = seq // tm
    est = 2 * _nbytes((tm, d), F32) + 2 * _nbytes((tm, d), BF16) + 4 * _nbytes((tm, d), F32)
    return pl.pallas_call(
        _ln_mod_body,
        grid=(t // tm,),
        in_specs=[pl.BlockSpec((tm, d), lambda i: (i, 0)),
                  pl.BlockSpec((None, 1, d), lambda i: (i // per_seq, 0, shift_idx)),
                  pl.BlockSpec((None, 1, d), lambda i: (i // per_seq, 0, scale_idx))],
        out_specs=pl.BlockSpec((tm, d), lambda i: (i, 0)),
        out_shape=jax.ShapeDtypeStruct((t, d), BF16),
        compiler_params=_params(("parallel",), est),
    )(x2, mod3, mod3)


def _res_ln_body(*refs, alpha, with_h):
    if with_h:
        x_ref, m_ref, gate_ref, g_ref, b_ref, sh_ref, sc_ref, o_ref, h_ref = refs
    else:
        x_ref, m_ref, gate_ref, g_ref, b_ref, o_ref = refs
    v = alpha * x_ref[...] + gate_ref[...] * m_ref[...].astype(F32)
    xn = _layer_norm(v) * g_ref[...] + b_ref[...]
    o_ref[...] = xn
    if with_h:
        h_ref[...] = (_layer_norm(xn) * (1.0 + sc_ref[...]) + sh_ref[...]).astype(h_ref.dtype)


def _residual_ln(x2, m2, mod3, gate_idx, ln_g, ln_b, seq, alpha, h_idx=None):
    t, d = x2.shape
    tm = _pick(seq, (256, 128, 64, 32, 16))
    per_seq = seq // tm
    with_h = h_idx is not None
    row = pl.BlockSpec((tm, d), lambda i: (i, 0))
    vec = pl.BlockSpec((1, d), lambda i: (0, 0))

    def modspec(k):
        return pl.BlockSpec((None, 1, d), lambda i: (i // per_seq, 0, k))

    in_specs = [row, row, modspec(gate_idx), vec, vec]
    args = [x2, m2, mod3, ln_g.reshape(1, d), ln_b.reshape(1, d)]
    out_specs = [row]
    out_shape = [jax.ShapeDtypeStruct((t, d), F32)]
    if with_h:
        in_specs += [modspec(h_idx[0]), modspec(h_idx[1])]
        args += [mod3, mod3]
        out_specs.append(row)
        out_shape.append(jax.ShapeDtypeStruct((t, d), BF16))
    est = 12 * _nbytes((tm, d), F32)
    outs = pl.pallas_call(
        functools.partial(_res_ln_body, alpha=alpha, with_h=with_h),
        grid=(t // tm,),
        in_specs=in_specs,
        out_specs=out_specs,
        out_shape=out_shape,
        compiler_params=_params(("parallel",), est),
    )(*args)
    return outs if with_h else outs[0]


def _shift_matrix(blk):
    i = jnp.arange(blk)[:, None]
    j = jnp.arange(blk)[None, :]
    return jnp.concatenate([j == i - 1, j == i + 1], axis=1).astype(BF16)


def _conv3(cur_ref, prev_ref, next_ref, w_ref, b_ref, shift, first, last, halo, rows):
    tm = cur_ref.shape[0]
    lo, hi = rows.start, rows.stop
    blk = hi - lo
    sub = V7X_F32_SUBLANES
    w = w_ref[...]
    w_side = w.astype(BF16)
    xb = cur_ref[lo:hi, :]
    side = jnp.concatenate([xb * w_side[0:1, :], xb * w_side[2:3, :]], axis=0)
    out = (jnp.dot(shift, side, preferred_element_type=F32)
           + ((xb * w_side[1:2, :]).astype(F32) + b_ref[...]))
    if lo == 0:
        before = prev_ref[halo - 1:halo, :].astype(F32) * jnp.where(first, 0.0, 1.0)
    else:
        before = cur_ref[lo - 1:lo, :].astype(F32)
    if hi == tm:
        after = next_ref[0:1, :].astype(F32) * jnp.where(last, 0.0, 1.0)
    else:
        after = cur_ref[hi:hi + 1, :].astype(F32)
    row = lax.broadcasted_iota(jnp.int32, (sub, xb.shape[1]), 0)
    head = out[0:sub] + jnp.where(row == 0, before * w[0:1, :], 0.0)
    tail = out[blk - sub:blk] + jnp.where(row == sub - 1, after * w[2:3, :], 0.0)
    return jnp.concatenate([head, out[sub:blk - sub], tail], axis=0)


def _conv_body(*refs, per_seq, glu, halo, blk):
    i = pl.program_id(1)
    first = (i % per_seq) == 0
    last = (i % per_seq) == per_seq - 1
    shift = refs[0][...]
    o_ref = refs[-1]
    tm = o_ref.shape[0]
    for lo in range(0, tm, blk):
        rows = slice(lo, lo + blk)
        if glu:
            ca, pa, na, wa, ba, cb, pb, nb, wb, bb = refs[1:-1]
            ua = _conv3(ca, pa, na, wa, ba, shift, first, last, halo, rows)
            ub = _conv3(cb, pb, nb, wb, bb, shift, first, last, halo, rows)
            o_ref[rows, :] = (_silu(ua) * ub).astype(o_ref.dtype)
        else:
            ca, pa, na, wa, ba = refs[1:-1]
            o_ref[rows, :] = _silu(_conv3(ca, pa, na, wa, ba, shift, first, last, halo, rows)).astype(o_ref.dtype)


def _dwconv_act(u, w, b, seq, glu):
    t, c_in = u.shape
    c_out = c_in // 2 if glu else c_in
    halo = V7X_BF16_SUBLANES
    tm = _pick(seq, (2048, 1024, 512, 256, 128, 64, 32, 16))
    tc = _pick(c_out, (1024, 512, 256, 128))
    blk = min(tm, V7X_MXU_DIM)
    per_seq = seq // tm
    hb = tm // halo
    n_halo = t // halo
    b2 = b.reshape(1, c_in)

    def group(off):
        return [pl.BlockSpec((tm, tc), lambda j, i: (i, j + off)),
                pl.BlockSpec((halo, tc), lambda j, i: (jnp.maximum(i * hb - 1, 0), j + off)),
                pl.BlockSpec((halo, tc), lambda j, i: (jnp.minimum((i + 1) * hb, n_halo - 1), j + off)),
                pl.BlockSpec((3, tc), lambda j, i: (0, j + off)),
                pl.BlockSpec((1, tc), lambda j, i: (0, j + off))]

    in_specs = [pl.BlockSpec((blk, 2 * blk), lambda j, i: (0, 0))] + group(0)
    args = [_shift_matrix(blk), u, u, u, w, b2]
    if glu:
        in_specs += group(c_out // tc)
        args += [u, u, u, w, b2]
    est = ((2 if glu else 1) * (2 * _nbytes((tm, tc), BF16) + 8 * _nbytes((blk, tc), F32))
           + 2 * _nbytes((tm, tc), BF16))
    return pl.pallas_call(
        functools.partial(_conv_body, per_seq=per_seq, glu=glu, halo=halo, blk=blk),
        grid=(c_out // tc, t // tm),
        in_specs=in_specs,
        out_specs=pl.BlockSpec((tm, tc), lambda j, i: (i, j)),
        out_shape=jax.ShapeDtypeStruct((t, c_out), BF16),
        compiler_params=_params(("parallel", "parallel"), est),
    )(*args)


def _split3_dot(tri, a):
    a1 = a.astype(BF16)
    r1 = a - a1.astype(F32)
    a2 = r1.astype(BF16)
    a3 = (r1 - a2.astype(F32)).astype(BF16)
    dot = functools.partial(jnp.dot, preferred_element_type=F32)
    return dot(tri, a1) + dot(tri, a2) + dot(tri, a3)


def _dt_body(raw_ref, bias_ref, alog_ref, cum_ref, src_ref, *, heads, groups):
    raw = raw_ref[...]
    v = raw + bias_ref[...]
    dt = jnp.maximum(v, 0.0) + jnp.log1p(jnp.exp(-jnp.abs(v)))
    a = dt * (-jnp.exp(alog_ref[...]))
    ln = raw.shape[0]
    r = lax.broadcasted_iota(jnp.int32, (ln, ln), 0)
    s = lax.broadcasted_iota(jnp.int32, (ln, ln), 1)
    lower = jnp.where(r >= s, 1.0, 0.0).astype(BF16)
    upper = jnp.where(r <= s, 1.0, 0.0).astype(BF16)
    cum = jnp.concatenate([_split3_dot(lower, a[:, :heads]), _split3_dot(upper, a[:, heads:])], axis=1)
    cum2 = cum * LOG2_E
    src_t = (cum2 - jnp.log(dt) * LOG2_E).T
    hg = heads // groups
    for d in range(2):
        for g in range(groups):
            lo = d * heads + g * hg
            cum_ref[d, g] = cum2[:, lo:lo + hg]
            src_ref[d, g, 0] = src_t[lo:lo + hg, :]


def _ssd_steps(dt_raw, dt_bias, a_log, groups):
    t, h2 = dt_raw.shape
    heads = h2 // 2
    hg = heads // groups
    ln = SSM_CHUNK
    nct = t // ln
    return pl.pallas_call(
        functools.partial(_dt_body, heads=heads, groups=groups),
        grid=(nct,),
        in_specs=[pl.BlockSpec((ln, h2), lambda c: (c, 0)),
                  pl.BlockSpec((1, h2), lambda c: (0, 0)),
                  pl.BlockSpec((1, h2), lambda c: (0, 0))],
        out_specs=[pl.BlockSpec((2, groups, ln, hg), lambda c: (0, 0, c, 0)),
                   pl.BlockSpec((2, groups, 1, hg, ln), lambda c: (0, 0, c, 0, 0))],
        out_shape=[jax.ShapeDtypeStruct((2, groups, t, hg), F32),
                   jax.ShapeDtypeStruct((2, groups, nct, hg, ln), F32)],
        compiler_params=_params(("parallel",), 0),
    )(dt_raw, dt_bias.reshape(1, h2), a_log.reshape(1, h2))


def _ssd_body(*refs, forward, groups, hg, p):
    if forward:
        x_ref, b_ref, c_ref, cum_ref, src_ref, yb_ref, z_ref, dsk_ref, w_ref, o_ref, h_ref, ch_ref = refs
    else:
        x_ref, b_ref, c_ref, cum_ref, src_ref, o_ref, h_ref, ch_ref = refs

    @pl.when(pl.program_id(1) == 0)
    def _():
        h_ref[...] = jnp.zeros_like(h_ref)

    ln = x_ref.shape[0]
    n = SSM_STATE
    width = 2 * p
    gw = hg * p
    r = lax.broadcasted_iota(jnp.int32, (ln, ln), 0)
    s = lax.broadcasted_iota(jnp.int32, (ln, ln), 1)
    mask = (r >= s) if forward else (r <= s)
    end_row = ln - 1 if forward else 0
    low = lax.broadcasted_iota(jnp.int32, (ln, width), 1) < p
    zero = jnp.zeros((ln, width), x_ref.dtype)

    def group(g, carry):
        gs = pl.ds(pl.multiple_of(g * gw, gw), gw)
        ns = pl.ds(pl.multiple_of(g * n, n), n)
        bm = b_ref[:, ns]
        cm = c_ref[:, ns]
        cb = lax.dot_general(cm, bm, (((1,), (1,)), ((), ())), preferred_element_type=F32)
        ch_ref[...] = jnp.dot(cm, h_ref[g].astype(BF16), preferred_element_type=F32)
        bm_t16 = bm.astype(F32).T.astype(BF16)
        cb16 = cb.astype(BF16)

        for i in range(hg // 2):
            ja, jb = 2 * i, 2 * i + 1
            sl = slice(i * width, (i + 1) * width)
            xs = pl.ds(pl.multiple_of(g * gw + i * width, width), width)
            cum_a = jnp.broadcast_to(cum_ref[g, :, ja:ja + 1], (ln, ln))
            cum_b = jnp.broadcast_to(cum_ref[g, :, jb:jb + 1], (ln, ln))
            src_a = src_ref[g, ja:ja + 1, :]
            src_b = src_ref[g, jb:jb + 1, :]
            m_a = cb16 * jnp.exp2(jnp.where(mask, cum_a - src_a, -jnp.inf)).astype(BF16)
            m_b = cb16 * jnp.exp2(jnp.where(mask, cum_b - src_b, -jnp.inf)).astype(BF16)
            xp = x_ref[:, xs]
            x_ab = jnp.concatenate([jnp.where(low, xp, zero), jnp.where(low, zero, xp)], axis=0)
            cum_e = jnp.where(low, cum_a, cum_b)
            y = (jnp.dot(jnp.concatenate([m_a, m_b], axis=1), x_ab, preferred_element_type=F32)
                 + jnp.exp2(cum_e) * ch_ref[:, sl])
            if forward:
                ch_ref[:, sl] = y
            else:
                o_ref[:, xs] = y.astype(o_ref.dtype)
            end_a = cum_ref[g, end_row:end_row + 1, ja:ja + 1]
            end_b = cum_ref[g, end_row:end_row + 1, jb:jb + 1]
            bt_ab = jnp.concatenate([bm_t16 * jnp.exp2(end_a - src_a).astype(BF16),
                                     bm_t16 * jnp.exp2(end_b - src_b).astype(BF16)], axis=1)
            h_ref[g, :, sl] = (h_ref[g, :, sl] * jnp.exp2(cum_e[end_row:end_row + 1, :])
                               + jnp.dot(bt_ab, x_ab, preferred_element_type=F32))

        if forward:
            y = ch_ref[...] + yb_ref[:, gs].astype(F32) + dsk_ref[:, gs] * x_ref[:, gs].astype(F32)
            yg = y * _silu(z_ref[:, gs].astype(F32))
            yg = yg * lax.rsqrt(jnp.mean(yg * yg, axis=-1, keepdims=True) + RMS_EPS)
            o_ref[:, gs] = (yg * w_ref[:, gs]).astype(o_ref.dtype)
        return carry

    lax.fori_loop(0, groups, group, 0, unroll=8)


def _ssd_scan(xbc_act, cum, src, bsz, seq, d_inner, groups, forward, epilogue_inputs=()):
    t = xbc_act.shape[0]
    n = SSM_STATE
    ln = SSM_CHUNK
    nc = seq // ln
    hg = cum.shape[3]
    heads = hg * groups
    p = d_inner // heads
    gw = hg * p
    gn = groups * n
    assert 2 * p == V7X_LANES and ln == V7X_LANES and hg % 2 == 0 and d_inner % gn == 0
    d = 0 if forward else 1

    def rb(b, c):
        return b * nc + (c if forward else nc - 1 - c)

    rows = pl.BlockSpec((ln, d_inner), lambda b, c: (rb(b, c), 0))
    vec = pl.BlockSpec((1, d_inner), lambda b, c: (0, 0))
    in_specs = [rows,
                pl.BlockSpec((ln, gn), lambda b, c: (rb(b, c), d_inner // gn)),
                pl.BlockSpec((ln, gn), lambda b, c: (rb(b, c), d_inner // gn + 1)),
                pl.BlockSpec((None, groups, ln, hg), lambda b, c: (d, 0, rb(b, c), 0)),
                pl.BlockSpec((None, groups, None, hg, ln), lambda b, c: (d, 0, rb(b, c), 0, 0))]
    args = [xbc_act, xbc_act, xbc_act, cum, src]
    if forward:
        y_bwd, z, d_skip_e, norm_w = epilogue_inputs
        in_specs += [rows, rows, vec, vec]
        args += [y_bwd, z, d_skip_e.reshape(1, d_inner), norm_w.reshape(1, d_inner)]
    est = (2 * (4 if forward else 2) * _nbytes((ln, d_inner), BF16) + 4 * _nbytes((ln, gn), BF16)
           + _nbytes((n, d_inner), F32) + 10 * _nbytes((ln, gw), F32) + 16 * _nbytes((ln, ln), F32))
    return pl.pallas_call(
        functools.partial(_ssd_body, forward=forward, groups=groups, hg=hg, p=p),
        grid=(bsz, nc),
        in_specs=in_specs,
        out_specs=rows,
        out_shape=jax.ShapeDtypeStruct((t, d_inner), BF16),
        scratch_shapes=[pltpu.VMEM((groups, n, gw), F32), pltpu.VMEM((ln, gw), F32)],
        compiler_params=_params(("parallel", "arbitrary"), est),
    )(*args)


def _rope_tables(seq, dh):
    half = dh // 2
    quarter = half // 2
    rows = seq // GRID_W
    row = jnp.repeat(jnp.arange(rows, dtype=jnp.int32), GRID_W).astype(F32)
    col = jnp.tile(jnp.arange(GRID_W, dtype=jnp.int32), rows).astype(F32)
    inv = 1.0 / (ROPE_THETA ** (jnp.arange(quarter, dtype=F32) / quarter))
    ang_r = row[:, None] * inv[None, :]
    ang_c = col[:, None] * inv[None, :]
    cos = jnp.concatenate([jnp.cos(ang_r), jnp.cos(ang_c), jnp.cos(ang_r), jnp.cos(ang_c)], axis=-1)
    sin = jnp.concatenate([-jnp.sin(ang_r), -jnp.sin(ang_c), jnp.sin(ang_r), jnp.sin(ang_c)], axis=-1)
    return cos, sin


def _pair_major(a, dh):
    lead = a.shape[:-1]
    nh = a.shape[-1] // dh
    a = a.reshape(lead + (nh, 2, 2, dh // 4))
    return jnp.swapaxes(a, -3, -2).reshape(lead + (nh * dh,))


def _norm_rope_epilogue(accs, extras, *, dh, scale):
    acc = accs[0]
    w, cos, sin = extras
    w = w * scale
    ones = jnp.ones((dh, dh), BF16)
    outs = []
    for hd in range(acc.shape[1] // dh):
        v = acc[:, hd * dh:(hd + 1) * dh]
        ss = jnp.dot((v * v).astype(BF16), ones, preferred_element_type=F32)
        v = v * lax.rsqrt(ss * (1.0 / dh) + RMS_EPS) * w
        outs.append(v * cos + pltpu.roll(v, dh // 2, 1) * sin)
    return [jnp.concatenate(outs, axis=1)]


def _proj_norm_rope(h, w, norm_w, cos, sin, seq, dh, scale, tm):
    n = w.shape[1]
    tn = _pick(n, (512, 256, 128))
    per_seq = seq // tm
    specs = [pl.BlockSpec((1, dh), lambda i, j: (0, 0)),
             pl.BlockSpec((tm, dh), lambda i, j: (i % per_seq, 0)),
             pl.BlockSpec((tm, dh), lambda i, j: (i % per_seq, 0))]
    (out,) = _fused_matmul([h], [_pair_major(w, dh)], [_pair_major(norm_w, dh).reshape(1, dh), cos, sin], specs,
                           functools.partial(_norm_rope_epilogue, dh=dh, scale=scale), [BF16], tm, tn)
    return out


def _attn_body(q_ref, k_ref, v_ref, o_ref, *, dh, kv_per_step):
    g_per_kv = q_ref.shape[1] // dh // kv_per_step
    for g in range(q_ref.shape[1] // dh):
        kv = g // g_per_kv
        k = k_ref[:, kv * dh:(kv + 1) * dh]
        v = v_ref[:, kv * dh:(kv + 1) * dh]
        sl = slice(g * dh, (g + 1) * dh)
        s = lax.dot_general(q_ref[:, sl], k, (((1,), (1,)), ((), ())), preferred_element_type=F32)
        e = jnp.exp2(s - jnp.max(s, axis=-1, keepdims=True))
        den = jnp.sum(e, axis=-1, keepdims=True)
        o = jnp.dot(e.astype(BF16), v, preferred_element_type=F32)
        o_ref[:, sl] = (o / den).astype(o_ref.dtype)


def _attention(q, k, v, bsz, seq, dh):
    t, q_cols = q.shape
    kv_heads = k.shape[1] // dh
    kv_per_step = 2 if kv_heads % 2 == 0 else 1
    gq = q_cols // kv_heads * kv_per_step
    tq = _pick(seq, (256, 128, 64, 32, 16))
    nq = seq // tq
    est = (4 * kv_per_step * _nbytes((seq, dh), BF16) + 4 * _nbytes((tq, gq), BF16)
           + 4 * _nbytes((tq, seq), F32) + 2 * _nbytes((tq, seq), BF16))
    return pl.pallas_call(
        functools.partial(_attn_body, dh=dh, kv_per_step=kv_per_step),
        grid=(bsz, kv_heads // kv_per_step, nq),
        in_specs=[pl.BlockSpec((tq, gq), lambda b, j, i: (b * nq + i, j)),
                  pl.BlockSpec((seq, kv_per_step * dh), lambda b, j, i: (b, j)),
                  pl.BlockSpec((seq, kv_per_step * dh), lambda b, j, i: (b, j))],
        out_specs=pl.BlockSpec((tq, gq), lambda b, j, i: (b * nq + i, j)),
        out_shape=jax.ShapeDtypeStruct((t, q_cols), BF16),
        compiler_params=_params(("parallel", "parallel", "arbitrary"), est),
    )(q, k, v)


def _plain(accs, extras):
    return [accs[0]]


def _gate_epilogue(accs, extras):
    return [_sigmoid(accs[0] + extras[0])]


def _mix_epilogue(accs, extras):
    return [extras[0].astype(F32) * accs[0] + extras[1].astype(F32) * accs[1]]


def kernel(x, c, w_ada, b_ada, w_in, ssm_conv_w, ssm_conv_b, ssm_a_log, ssm_dt_bias, ssm_d, ssm_norm_w,
           q_norm_w, k_norm_w, w_ssm_proj, w_attn_proj, w_gate, b_gate, w_out, ln1_g, ln1_b, w_up,
           ffn_conv_w, ffn_conv_b, w_down, ln2_g, ln2_b):
    bsz, seq, d = x.shape
    depth = w_ada.shape[0]
    t = bsz * seq
    d_inner = w_ssm_proj.shape[1]
    heads = ssm_a_log.shape[2]
    conv_dim = ssm_conv_w.shape[2]
    groups = (conv_dim - d_inner) // (2 * SSM_STATE)
    q_cols = w_attn_proj.shape[1]
    dh = q_norm_w.shape[1]
    kv_cols = (w_in.shape[2] - d_inner - conv_dim - 2 * heads - q_cols) // 2
    ffn = w_down.shape[1]
    alpha = (2.0 * depth) ** 0.25
    o_xbc = d_inner
    o_dt = o_xbc + conv_dim
    o_q = o_dt + 2 * heads
    o_k = o_q + q_cols
    o_v = o_k + kv_cols

    tm = _pick(seq, (1024, 512, 256, 128))
    cos, sin = _rope_tables(seq, dh)
    x2 = x.reshape(t, d)

    for layer in range(depth):
        wi = w_in[layer].astype(BF16)
        w_q = wi[:, o_q:o_k]
        w_k = wi[:, o_k:o_v]

        mod3 = _ada_mod(c, w_ada[layer], b_ada[layer]).reshape(bsz, 1, 6 * d)
        h = _ln_modulate(x2, mod3, seq, 0, 1)

        def proj(col0, width, dtype):
            tn = _pick(math.gcd(col0, width), (1024, 512, 256, 128))
            return _fused_matmul([h], [wi], [], [], _plain, [dtype], tm, tn, b_cols=(col0, width))[0]

        z = proj(0, o_xbc, BF16)
        xbc = proj(o_xbc, conv_dim, BF16)
        dt_raw = proj(o_dt, 2 * heads, F32)
        v = _fused_matmul([h], [wi[:, o_v:]], [], [], _plain, [BF16], tm,
                          _pick(kv_cols, (1024, 512, 256, 128)))[0]
        q = _proj_norm_rope(h, w_q, q_norm_w[layer], cos, sin, seq, dh, dh ** -0.5 * LOG2_E, tm)
        k = _proj_norm_rope(h, w_k, k_norm_w[layer], cos, sin, seq, dh, 1.0, tm)

        xbc_act = _dwconv_act(xbc, ssm_conv_w[layer], ssm_conv_b[layer], seq, glu=False)
        cum, src = _ssd_steps(dt_raw, ssm_dt_bias[layer], ssm_a_log[layer], groups)
        y_bwd = _ssd_scan(xbc_act, cum, src, bsz, seq, d_inner, groups, forward=False)
        d_skip_e = jnp.repeat(ssm_d[layer], d_inner // heads)
        y_ssm = _ssd_scan(xbc_act, cum, src, bsz, seq, d_inner, groups, forward=True,
                          epilogue_inputs=(y_bwd, z, d_skip_e, ssm_norm_w[layer]))

        y_attn = _attention(q, k, v, bsz, seq, dh)

        tn_d = _pick(d, (512, 256, 128))
        tn_wide = _pick(d, (1024, 512, 256, 128))
        wg = w_gate[layer].astype(BF16)
        bg = b_gate[layer].reshape(1, 2 * d)
        (gates,) = _fused_matmul([h], [wg], [bg], [pl.BlockSpec((1, tn_wide), lambda i, j: (0, j))],
                                 _gate_epilogue, [BF16], tm, tn_wide)
        nb = d // tn_d
        w_sp = w_ssm_proj[layer].astype(BF16)
        w_ap = w_attn_proj[layer].astype(BF16)
        tm_mix = _fit_tm(tm, tn_d, [y_ssm, y_attn], [w_sp, w_ap], 2, [BF16])
        (mixed_pre,) = _fused_matmul(
            [y_ssm, y_attn], [w_sp, w_ap], [gates, gates],
            [pl.BlockSpec((tm_mix, tn_d), lambda i, j: (i, j)),
             pl.BlockSpec((tm_mix, tn_d), lambda i, j: (i, j + nb))],
            _mix_epilogue, [BF16], tm_mix, tn_d)
        mixed = _fused_matmul([mixed_pre], [w_out[layer].astype(BF16)], [], [], _plain, [BF16], tm, tn_wide)[0]
        x2, h2 = _residual_ln(x2, mixed, mod3, 2, ln1_g[layer], ln1_b[layer], seq, alpha, h_idx=(3, 4))

        w_up_b = w_up[layer].astype(BF16)
        tn_up = _pick(2 * ffn, (1024, 512, 256, 128))
        tm_up = _fit_tm(_pick(seq, (2048, 1024, 512, 256, 128)), tn_up, [h2], [w_up_b], 0, [BF16])
        u = _fused_matmul([h2], [w_up_b], [], [], _plain, [BF16], tm_up, tn_up)[0]
        act = _dwconv_act(u, ffn_conv_w[layer], ffn_conv_b[layer], seq, glu=True)
        w_dn = w_down[layer].astype(BF16)
        tm_dn = _fit_tm(tm, tn_d, [act], [w_dn], 0, [BF16])
        f = _fused_matmul([act], [w_dn], [], [], _plain, [BF16], tm_dn, tn_d)[0]
        x2 = _residual_ln(x2, f, mod3, 5, ln2_g[layer], ln2_b[layer], seq, alpha)

    return x2.reshape(bsz, seq, d)
```
